```python
import math
import jax, jax.numpy as jnp
from jax import lax
import numpy as np

D_MODEL = 2048
BATCH = 16
SEQ = 2048
DEPTH = 1

CHUNK = 64
Q_BLOCK = 128

A_WIDTH = D_MODEL // 2
A_HEAD_DIM = 64
A_HEADS = A_WIDTH // A_HEAD_DIM
DECAY_LORA = max(32, int(round(1.8 * D_MODEL ** 0.5 / 32)) * 32)
AAA_LORA = max(32, int(round(1.8 * D_MODEL ** 0.5 / 32)) * 32)
GATE_LORA = max(32, int(round(0.6 * D_MODEL ** 0.8 / 32)) * 32)
GN_EPS = 64e-5

B_QK_DIM = 64
B_V_DIM = 2 * B_QK_DIM
B_HEADS = (D_MODEL // 2) // B_V_DIM
B_WIDTH = B_HEADS * B_V_DIM
SUBLN_EPS = 1e-5
N_BUCKETS = 32
MAX_DISTANCE = 128
NEG_INF = -1e30

D_FF = 4 * D_MODEL
RMS_EPS = 1e-6

A_COLS = 3 * A_WIDTH + DECAY_LORA + AAA_LORA + GATE_LORA
B_QK_COLS = 2 * B_HEADS * B_QK_DIM
B_COLS = 2 * B_QK_COLS + B_WIDTH
GATE_COLS = 2 * D_MODEL
IN_COLS = A_COLS + B_COLS + GATE_COLS

kernel_name = "hybrid_rwkv7_diffattn_gated_block"


def rmsnorm(x, g, eps=RMS_EPS):
    xf = x.astype(jnp.float32)
    y = xf * lax.rsqrt(jnp.mean(xf * xf, axis=-1, keepdims=True) + eps)
    return (y * g.astype(jnp.float32)).astype(x.dtype)


def token_shift(z):
    return jnp.pad(z, ((0, 0), (1, 0), (0, 0)))[:, :-1]


def t5_bucket(rel):
    nb = N_BUCKETS // 2
    max_exact = nb // 2
    ret = jnp.where(rel > 0, nb, 0)
    n = jnp.abs(rel)
    nf = jnp.maximum(n, 1).astype(jnp.float32)
    large = max_exact + (jnp.log(nf / max_exact) / math.log(MAX_DISTANCE / max_exact)
                         * (nb - max_exact)).astype(jnp.int32)
    large = jnp.minimum(large, nb - 1)
    return ret + jnp.where(n < max_exact, n, large)


def rwkv7_branch(zA, mu, w0, w_up, a0, a_up, g_up, k_k, k_a, r_k, lnx_g, lnx_b):
    f32 = jnp.float32
    Bz, S, _ = zA.shape
    zs = zA + (token_shift(zA) - zA) * mu
    cut = [A_WIDTH, 2 * A_WIDTH, 3 * A_WIDTH, 3 * A_WIDTH + DECAY_LORA, 3 * A_WIDTH + DECAY_LORA + AAA_LORA]
    r, k, v, wd, ad, gd = jnp.split(zs, cut, axis=-1)
    w = -jax.nn.softplus(-(w0 + jnp.tanh(wd) @ w_up).astype(f32)) - 0.5
    decay = jnp.exp(-jnp.exp(w))
    a = jax.nn.sigmoid((a0 + ad @ a_up).astype(f32))
    g = jax.nn.sigmoid(gd) @ g_up

    def heads(t):
        return t.astype(f32).reshape(Bz, S, A_HEADS, A_HEAD_DIM)

    r, k, v, decay, a = heads(r), heads(k), heads(v), heads(decay), heads(a)
    kk = k * k_k.astype(f32).reshape(A_HEADS, A_HEAD_DIM)
    kk = kk * lax.rsqrt(jnp.maximum(jnp.sum(kk * kk, axis=-1, keepdims=True), 1e-24))
    k = k * (1.0 + (a - 1.0) * k_a.astype(f32).reshape(A_HEADS, A_HEAD_DIM))

    xs = tuple(jnp.moveaxis(t, 1, 0) for t in (r, decay, k, v, -kk, kk * a))

    def step(state, inp):
        r_t, w_t, k_t, v_t, a_t, b_t = inp
        sa = jnp.einsum('bhvk,bhk->bhv', state, a_t)
        state = (state * w_t[:, :, None, :] + sa[..., None] * b_t[:, :, None, :]
                 + v_t[..., None] * k_t[:, :, None, :])
        return state, jnp.einsum('bhvk,bhk->bhv', state, r_t)

    s0 = jnp.zeros((Bz, A_HEADS, A_HEAD_DIM, A_HEAD_DIM), f32)
    _, y = lax.scan(step, s0, xs)
    y = jnp.moveaxis(y, 0, 1)
    mean = jnp.mean(y, axis=-1, keepdims=True)
    var = jnp.mean(jnp.square(y - mean), axis=-1, keepdims=True)
    y = ((y - mean) * lax.rsqrt(var + GN_EPS)).reshape(Bz, S, A_WIDTH)
    y = y * lnx_g.astype(f32) + lnx_b.astype(f32)
    bonus = jnp.sum(r * k * r_k.astype(f32), axis=-1, keepdims=True) * v
    y = y + bonus.reshape(Bz, S, A_WIDTH)
    return y.astype(zA.dtype) * g


def diff_attention_branch(zB, lq1, lk1, lq2, lk2, subln_g, rel_bias, lambda_init):
    f32 = jnp.float32
    Bz, S, _ = zB.shape
    q, k, v = jnp.split(zB, [B_QK_COLS, 2 * B_QK_COLS], axis=-1)
    q = q.reshape(Bz, S, 2 * B_HEADS, B_QK_DIM)
    k = k.reshape(Bz, S, 2 * B_HEADS, B_QK_DIM)
    v = v.reshape(Bz, S, B_HEADS, B_V_DIM)
    lam = (jnp.exp(jnp.sum(lq1.astype(f32) * lk1.astype(f32)))
           - jnp.exp(jnp.sum(lq2.astype(f32) * lk2.astype(f32))) + lambda_init)
    scale = B_QK_DIM ** -0.5
    table = rel_bias.astype(f32)
    outs = []
    for start in range(0, S, Q_BLOCK):
        end = start + Q_BLOCK
        qpos = jnp.arange(start, end)
        kpos = jnp.arange(end)
        logits = jnp.einsum('bqhd,bkhd->bhqk', q[:, start:end], k[:, :end]).astype(f32) * scale
        bias = table[t5_bucket(kpos[None, :] - qpos[:, None])]
        logits = logits + jnp.transpose(bias, (2, 0, 1))[None]
        allowed = (kpos[None, :] // CHUNK) <= (qpos[:, None] // CHUNK)
        logits = jnp.where(allowed, logits, NEG_INF)
        p = jax.nn.softmax(logits, axis=-1).reshape(Bz, B_HEADS, 2, Q_BLOCK, end)
        attn = p[:, :, 0] - lam * p[:, :, 1]
        outs.append(jnp.einsum('bhqk,bkhe->bqhe', attn.astype(v.dtype), v[:, :end]))
    o = jnp.concatenate(outs, axis=1).astype(f32)
    o = o * lax.rsqrt(jnp.mean(o * o, axis=-1, keepdims=True) + SUBLN_EPS) * subln_g.astype(f32)
    o = o * (1.0 - lambda_init)
    return o.reshape(Bz, S, B_WIDTH).astype(zB.dtype)


def setup_inputs(seed: int = 0) -> dict:
    key = jax.random.key(seed)
    ks = jax.random.split(key, 27)
    f32 = jnp.float32
    L = DEPTH

    def nrm(k, shape, scale):
        return jax.random.normal(k, shape, f32) * scale

    return {
        "x": nrm(ks[0], (BATCH, SEQ, D_MODEL), 1.0),
        "norm_mix_g": 1.0 + nrm(ks[1], (L, D_MODEL), 0.02),
        "w_in": nrm(ks[2], (L, D_MODEL, IN_COLS), D_MODEL ** -0.5),
        "mu_shift": jax.random.uniform(ks[3], (L, A_COLS), f32),
        "w0": jax.random.uniform(ks[4], (L, A_WIDTH), f32, -6.0, -1.0),
        "w_up": nrm(ks[5], (L, DECAY_LORA, A_WIDTH), 0.5 * DECAY_LORA ** -0.5),
        "a0": nrm(ks[6], (L, A_WIDTH), 0.5),
        "a_up": nrm(ks[7], (L, AAA_LORA, A_WIDTH), AAA_LORA ** -0.5),
        "g_up": nrm(ks[8], (L, GATE_LORA, A_WIDTH), GATE_LORA ** -0.5),
        "k_k": 0.85 + nrm(ks[9], (L, A_WIDTH), 0.05),
        "k_a": 1.0 + nrm(ks[10], (L, A_WIDTH), 0.05),
        "r_k": nrm(ks[11], (L, A_HEADS, A_HEAD_DIM), 0.1),
        "lnx_g": 1.0 + nrm(ks[12], (L, A_WIDTH), 0.02),
        "lnx_b": nrm(ks[13], (L, A_WIDTH), 0.02),
        "lambda_q1": nrm(ks[14], (L, B_QK_DIM), 0.1),
        "lambda_k1": nrm(ks[15], (L, B_QK_DIM), 0.1),
        "lambda_q2": nrm(ks[16], (L, B_QK_DIM), 0.1),
        "lambda_k2": nrm(ks[17], (L, B_QK_DIM), 0.1),
        "subln_g": 1.0 + nrm(ks[18], (L, B_V_DIM), 0.02),
        "rel_bias": nrm(ks[19], (N_BUCKETS, 2 * B_HEADS), 0.2),
        "p_a": nrm(ks[20], (L, A_WIDTH, D_MODEL), A_WIDTH ** -0.5),
        "p_b": nrm(ks[21], (L, B_WIDTH, D_MODEL), B_WIDTH ** -0.5),
        "w_out": nrm(ks[22], (L, D_MODEL, D_MODEL), D_MODEL ** -0.5),
        "norm_mlp_g": 1.0 + nrm(ks[23], (L, D_MODEL), 0.02),
        "w_ff1": nrm(ks[24], (L, D_MODEL, D_FF), D_MODEL ** -0.5),
        "w_ff2": nrm(ks[25], (L, D_FF, D_MODEL), D_FF ** -0.5),
        "norm_final_g": 1.0 + nrm(ks[26], (D_MODEL,), 0.02),
    }


def reference(x, norm_mix_g, w_in, mu_shift, w0, w_up, a0, a_up, g_up, k_k, k_a, r_k, lnx_g, lnx_b,
              lambda_q1, lambda_k1, lambda_q2, lambda_k2, subln_g, rel_bias, p_a, p_b, w_out,
              norm_mlp_g, w_ff1, w_ff2, norm_final_g):
    h = x
    for l in range(DEPTH):
        lambda_init = 0.8 - 0.6 * math.exp(-0.3 * l)
        u = rmsnorm(h, norm_mix_g[l])
        z = u @ w_in[l]
        zA, zB, zg = jnp.split(z, [A_COLS, A_COLS + B_COLS], axis=-1)
        oA = rwkv7_branch(zA, mu_shift[l], w0[l], w_up[l], a0[l], a_up[l], g_up[l],
                          k_k[l], k_a[l], r_k[l], lnx_g[l], lnx_b[l]) @ p_a[l]
        oB = diff_attention_branch(zB, lambda_q1[l], lambda_k1[l], lambda_q2[l], lambda_k2[l],
                                   subln_g[l], rel_bias, lambda_init) @ p_b[l]
        gA, gB = jnp.split(zg, 2, axis=-1)
        merged = jax.nn.sigmoid(gA) * oA + jax.nn.sigmoid(gB) * oB
        h = h + merged @ w_out[l]
        m = rmsnorm(h, norm_mlp_g[l])
        h = h + jnp.square(jax.nn.relu(m @ w_ff1[l])) @ w_ff2[l]
    return rmsnorm(h, norm_final_g)
```

```python
import functools
import math

import jax
import jax.numpy as jnp
from jax import lax
from jax.experimental import pallas as pl
from jax.experimental.pallas import tpu as pltpu

F32 = jnp.float32
BF16 = jnp.bfloat16

LANES = 128
HEAD = 64
CHUNK = 64
Q_BLOCK = 128
RMS_EPS = 1e-6
GN_EPS = 64e-5
SUBLN_EPS = 1e-5
N_BUCKETS = 32
MAX_DISTANCE = 128
NEG_INF = -1e30
VMEM_LIMIT = 56 * 1024 * 1024
HI = lax.Precision.HIGHEST


def _mm(a, b):
    return jnp.dot(a.astype(BF16), b.astype(BF16), preferred_element_type=F32)


def _mm_nt(a, b):
    return lax.dot_general(a.astype(BF16), b.astype(BF16), (((1,), (1,)), ((), ())),
                           preferred_element_type=F32)


def _mm_hi(a, b):
    return jnp.dot(a, b, preferred_element_type=F32, precision=HI)


def _params(sem):
    return pltpu.CompilerParams(dimension_semantics=sem, vmem_limit_bytes=VMEM_LIMIT)


def _norm_matmul_kernel(x_ref, g_ref, w_ref, o_ref, u_ref, *, relu2):
    @pl.when(pl.program_id(1) == 0)
    def _():
        x = x_ref[...]
        ms = jnp.mean(x * x, axis=-1, keepdims=True)
        u_ref[...] = (x * lax.rsqrt(ms + RMS_EPS) * g_ref[...]).astype(BF16)

    acc = jnp.dot(u_ref[...], w_ref[...], preferred_element_type=F32)
    if relu2:
        acc = jnp.square(jnp.maximum(acc, 0.0))
    o_ref[...] = acc.astype(o_ref.dtype)


def _norm_matmul(x, g, w, *, tm, tn, relu2, out_dtype):
    m, d = x.shape
    n = w.shape[1]
    return pl.pallas_call(
        functools.partial(_norm_matmul_kernel, relu2=relu2),
        grid=(m // tm, n // tn),
        in_specs=[pl.BlockSpec((tm, d), lambda i, j: (i, 0)),
                  pl.BlockSpec((1, d), lambda i, j: (0, 0)),
                  pl.BlockSpec((d, tn), lambda i, j: (0, j))],
        out_specs=pl.BlockSpec((tm, tn), lambda i, j: (i, j)),
        out_shape=jax.ShapeDtypeStruct((m, n), out_dtype),
        scratch_shapes=[pltpu.VMEM((tm, d), BF16)],
        compiler_params=_params(("parallel", "arbitrary")),
        name="norm_matmul_relu2" if relu2 else "norm_matmul",
    )(x, g, w)


def _expand(t, lane_lo):
    zero = jnp.zeros_like(t)
    return jnp.concatenate([jnp.where(lane_lo, t, zero), jnp.where(lane_lo, zero, t)], axis=0)


def _rwkv_kernel(zr_ref, zk_ref, zv_ref, zl_ref, mur_ref, muk_ref, muv_ref, mul_ref,
                 w0_ref, wup_ref, a0_ref, aup_ref, gup_ref, kk_ref, ka_ref, rk_ref, lng_ref, lnb_ref,
                 o_ref, cr_ref, ck_ref, cv_ref, cl_ref, h_ref, *, n_chunks):
    c = CHUNK
    c2 = 2 * c

    @pl.when(pl.program_id(2) == 0)
    def _():
        cr_ref[...] = jnp.zeros_like(cr_ref)
        ck_ref[...] = jnp.zeros_like(ck_ref)
        cv_ref[...] = jnp.zeros_like(cv_ref)
        cl_ref[...] = jnp.zeros_like(cl_ref)
        h_ref[...] = jnp.zeros_like(h_ref)

    def shifted(z, carry_ref, mu):
        rows = lax.broadcasted_iota(jnp.int32, z.shape, 0)
        prev = jnp.where(rows == 0, carry_ref[...], pltpu.roll(z, 1, 0))
        carry_ref[...] = z[z.shape[0] - 1:, :]
        return z + (prev - z) * mu

    ri = lax.broadcasted_iota(jnp.int32, (c2, c2), 0)
    ci = lax.broadcasted_iota(jnp.int32, (c2, c2), 1)
    strict = ci < ri
    incl = ci <= ri
    eye = (ci == ri).astype(F32)
    head_ones = ((ri // HEAD) == (ci // HEAD)).astype(F32)
    tri = (lax.broadcasted_iota(jnp.int32, (c, c), 1)
           <= lax.broadcasted_iota(jnp.int32, (c, c), 0)).astype(F32)
    lane_lo = lax.broadcasted_iota(jnp.int32, (c, LANES), 1) < HEAD

    r_all = shifted(zr_ref[0], cr_ref, mur_ref[...])
    k_all = shifted(zk_ref[0], ck_ref, muk_ref[...])
    v_all = shifted(zv_ref[0], cv_ref, muv_ref[...])
    l_all = shifted(zl_ref[0], cl_ref, mul_ref[...])

    h = h_ref[...]
    for ch in range(n_chunks):
        sl = slice(ch * c, (ch + 1) * c)
        r, k, v, lo = r_all[sl], k_all[sl], v_all[sl], l_all[sl]
        t_w = w0_ref[...] + _mm(jnp.tanh(lo[:, 0:LANES]), wup_ref[...])
        lw = -math.exp(-0.5) * jax.nn.sigmoid(t_w)
        a_sig = jax.nn.sigmoid(a0_ref[...] + _mm(lo[:, LANES:2 * LANES], aup_ref[...]))
        gate = _mm(jax.nn.sigmoid(lo[:, 2 * LANES:]), gup_ref[...])

        kk = k * kk_ref[...]
        kk = kk * lax.rsqrt(jnp.maximum(_mm_hi(kk * kk, head_ones), 1e-24))
        k2 = k * (1.0 + (a_sig - 1.0) * ka_ref[...])
        a_v = -kk
        b_v = kk * a_sig

        cum = _mm_hi(tri, lw)
        cum_end = cum[c - 1:, :]
        g_in = jnp.exp(cum)
        g_inv = jnp.exp(-cum)
        g_tail = jnp.exp(cum_end - cum)
        rt = _expand(r * g_in, lane_lo)
        at = _expand(a_v * jnp.exp(cum - lw), lane_lo)
        kt = _expand(k2 * g_inv, lane_lo)
        bt = _expand(b_v * g_inv, lane_lo)
        kb = _expand(k2 * g_tail, lane_lo)
        bb = _expand(b_v * g_tail, lane_lo)
        ve = _expand(v, lane_lo)

        sc = _mm_nt(jnp.concatenate([at, rt], axis=0), jnp.concatenate([bt, kt], axis=0))
        n_ab = jnp.where(strict, sc[:c2, :c2], 0.0)
        a_ak = jnp.where(strict, sc[:c2, c2:], 0.0)
        a_rb = jnp.where(incl, sc[c2:, :c2], 0.0)
        a_rk = jnp.where(incl, sc[c2:, c2:], 0.0)

        p = eye + n_ab
        nk = _mm(n_ab, n_ab)
        power = 2
        while power < c:
            if 2 * power < c:
                both = _mm(jnp.concatenate([nk, p], axis=0), nk)
                nk, p = both[:c2], p + both[c2:]
            else:
                p = p + _mm(p, nk)
            power *= 2

        akv = _mm(a_ak, ve)
        ta = _mm(p, jnp.concatenate([at, akv], axis=1))
        rb = _mm(a_rb, ta)
        r_hat = rt + rb[:, :LANES]
        y_in = rb[:, LANES:] + _mm(a_rk, ve)
        mg = _mm(bb.T, ta)
        m_state = eye * jnp.exp(cum_end) + mg[:, :LANES]
        g_state = mg[:, LANES:] + _mm(kb.T, ve)

        ye = _mm(r_hat, h) + y_in
        h = _mm(m_state, h) + g_state
        y = ye[:c] + ye[c:]

        mean = _mm_hi(y, head_ones) * (1.0 / HEAD)
        d = y - mean
        var = _mm_hi(d * d, head_ones) * (1.0 / HEAD)
        yn = d * lax.rsqrt(var + GN_EPS) * lng_ref[...] + lnb_ref[...]
        bonus = _mm_hi(r * k2 * rk_ref[...], head_ones) * v
        o_ref[0, sl, :] = ((yn + bonus) * gate).astype(o_ref.dtype)
    h_ref[...] = h


def _rwkv_branch(z3, mu_r, mu_k, mu_v, mu_l, w0, wup, a0, aup, gup, k_k, k_a, r_k, lnx_g, lnx_b,
                 *, a_width, lora_block, ts, out_dtype):
    b, s, _ = z3.shape
    n_pairs = a_width // LANES
    wb = a_width // LANES
    lw = 4 * LANES

    def col(off):
        return lambda bi, p, si: (bi, si, off + p)

    def vec():
        return pl.BlockSpec((1, LANES), lambda bi, p, si: (0, p))

    in_specs = [
        pl.BlockSpec((1, ts, LANES), col(0)),
        pl.BlockSpec((1, ts, LANES), col(wb)),
        pl.BlockSpec((1, ts, LANES), col(2 * wb)),
        pl.BlockSpec((1, ts, lw), lambda bi, p, si: (bi, si, lora_block)),
        vec(), vec(), vec(),
        pl.BlockSpec((1, lw), lambda bi, p, si: (0, 0)),
        vec(),
        pl.BlockSpec((LANES, LANES), lambda bi, p, si: (0, p)),
        vec(),
        pl.BlockSpec((LANES, LANES), lambda bi, p, si: (0, p)),
        pl.BlockSpec((2 * LANES, LANES), lambda bi, p, si: (0, p)),
        vec(), vec(), vec(), vec(), vec(),
    ]
    return pl.pallas_call(
        functools.partial(_rwkv_kernel, n_chunks=ts // CHUNK),
        grid=(b, n_pairs, s // ts),
        in_specs=in_specs,
        out_specs=pl.BlockSpec((1, ts, LANES), lambda bi, p, si: (bi, si, p)),
        out_shape=jax.ShapeDtypeStruct((b, s, a_width), out_dtype),
        scratch_shapes=[pltpu.VMEM((1, LANES), F32), pltpu.VMEM((1, LANES), F32),
                        pltpu.VMEM((1, LANES), F32), pltpu.VMEM((1, lw), F32),
                        pltpu.VMEM((LANES, LANES), F32)],
        compiler_params=_params(("parallel", "parallel", "arbitrary")),
        name="rwkv7_chunked",
    )(z3, z3, z3, z3, mu_r, mu_k, mu_v, mu_l, w0, wup, a0, aup, gup, k_k, k_a, r_k, lnx_g, lnx_b)


def _attn_kernel(q_ref, k_ref, v_ref, bias_ref, lq1_ref, lk1_ref, lq2_ref, lk2_ref, sg_ref, o_ref,
                 *, n_qblocks, lambda_init):
    qb_rows = Q_BLOCK
    lam = (jnp.exp(jnp.sum(lq1_ref[...] * lk1_ref[...])) - jnp.exp(jnp.sum(lq2_ref[...] * lk2_ref[...]))
           + lambda_init)
    lane_lo = lax.broadcasted_iota(jnp.int32, (qb_rows, LANES), 1) < HEAD
    scale = HEAD ** -0.5

    def q_step(qb, _):
        q = q_ref[0, pl.ds(pl.multiple_of(qb * qb_rows, qb_rows), qb_rows), :] * scale
        qe = _expand(q, lane_lo).astype(BF16)

        def k_step(kb, carry):
            m, l, acc = carry
            rows = pl.ds(pl.multiple_of(kb * qb_rows, qb_rows), qb_rows)
            s = _mm_nt(qe, k_ref[0, rows, :]) + bias_ref[0, jnp.minimum(qb - kb, 2)]
            m_new = jnp.maximum(m, jnp.max(s, axis=-1, keepdims=True))
            alpha = jnp.exp(m - m_new)
            p = jnp.exp(s - m_new)
            l = alpha * l + jnp.sum(p, axis=-1, keepdims=True)
            acc = alpha * acc + _mm(p, v_ref[0, rows, :])
            return m_new, l, acc

        init = (jnp.full((2 * qb_rows, 1), -jnp.inf, F32), jnp.zeros((2 * qb_rows, 1), F32),
                jnp.zeros((2 * qb_rows, LANES), F32))
        _, l, acc = lax.fori_loop(0, qb + 1, k_step, init)
        o = acc / l
        o = o[:qb_rows] - lam * o[qb_rows:]
        o = o * lax.rsqrt(jnp.mean(o * o, axis=-1, keepdims=True) + SUBLN_EPS) * sg_ref[...]
        o = o * (1.0 - lambda_init)
        o_ref[0, pl.ds(pl.multiple_of(qb * qb_rows, qb_rows), qb_rows), :] = o.astype(o_ref.dtype)
        return 0

    lax.fori_loop(0, n_qblocks, q_step, 0)


def _attn_branch(z3, bias, lq1, lk1, lq2, lk2, subln_g, *, q_block, n_heads, lambda_init, out_dtype):
    b, s, _ = z3.shape

    def col(off):
        return lambda bi, hh: (bi, 0, off + hh)

    def small(n):
        return pl.BlockSpec((1, n), lambda bi, hh: (0, 0))

    return pl.pallas_call(
        functools.partial(_attn_kernel, n_qblocks=s // Q_BLOCK, lambda_init=lambda_init),
        grid=(b, n_heads),
        in_specs=[pl.BlockSpec((1, s, LANES), col(q_block)),
                  pl.BlockSpec((1, s, LANES), col(q_block + n_heads)),
                  pl.BlockSpec((1, s, LANES), col(q_block + 2 * n_heads)),
                  pl.BlockSpec((1, 3, 2 * Q_BLOCK, Q_BLOCK), lambda bi, hh: (hh, 0, 0, 0)),
                  small(HEAD), small(HEAD), small(HEAD), small(HEAD), small(LANES)],
        out_specs=pl.BlockSpec((1, s, LANES), lambda bi, hh: (bi, 0, hh)),
        out_shape=jax.ShapeDtypeStruct((b, s, n_heads * LANES), out_dtype),
        compiler_params=_params(("parallel", "parallel")),
        name="diff_attention",
    )(z3, z3, z3, bias, lq1, lk1, lq2, lk2, subln_g)


def _t5_bucket(rel):
    nb = N_BUCKETS // 2
    max_exact = nb // 2
    ret = jnp.where(rel > 0, nb, 0)
    n = jnp.abs(rel)
    nf = jnp.maximum(n, 1).astype(jnp.float32)
    large = max_exact + (jnp.log(nf / max_exact) / math.log(MAX_DISTANCE / max_exact)
                         * (nb - max_exact)).astype(jnp.int32)
    large = jnp.minimum(large, nb - 1)
    return ret + jnp.where(n < max_exact, n, large)


def _bias_tiles(rel_bias, n_heads):
    qi = jnp.arange(Q_BLOCK)[:, None]
    kj = jnp.arange(Q_BLOCK)[None, :]
    table = rel_bias.astype(F32)
    tiles = []
    for dist in range(3):
        rel = kj - qi - dist * Q_BLOCK
        t = jnp.transpose(table[_t5_bucket(rel)], (2, 0, 1))
        if dist == 0:
            t = jnp.where(((kj // CHUNK) <= (qi // CHUNK))[None], t, NEG_INF)
        tiles.append(t.reshape(n_heads, 2 * Q_BLOCK, Q_BLOCK))
    return jnp.stack(tiles, axis=1)


def _merge_kernel(ya_ref, ob_ref, pa_ref, pb_ref, ga_ref, gb_ref, o_ref):
    oa = jnp.dot(ya_ref[...].astype(BF16), pa_ref[...], preferred_element_type=F32)
    ob = jnp.dot(ob_ref[...].astype(BF16), pb_ref[...], preferred_element_type=F32)
    o_ref[...] = (jax.nn.sigmoid(ga_ref[...]) * oa + jax.nn.sigmoid(gb_ref[...]) * ob).astype(o_ref.dtype)


def _merge(ya, ob, p_a, p_b, z2, *, ga_block, gb_block, tm, tn, out_dtype):
    m, ka = ya.shape
    kb = ob.shape[1]
    n = p_a.shape[1]
    return pl.pallas_call(
        _merge_kernel,
        grid=(m // tm, n // tn),
        in_specs=[pl.BlockSpec((tm, ka), lambda i, j: (i, 0)),
                  pl.BlockSpec((tm, kb), lambda i, j: (i, 0)),
                  pl.BlockSpec((ka, tn), lambda i, j: (0, j)),
                  pl.BlockSpec((kb, tn), lambda i, j: (0, j)),
                  pl.BlockSpec((tm, tn), lambda i, j: (i, ga_block + j)),
                  pl.BlockSpec((tm, tn), lambda i, j: (i, gb_block + j))],
        out_specs=pl.BlockSpec((tm, tn), lambda i, j: (i, j)),
        out_shape=jax.ShapeDtypeStruct((m, n), out_dtype),
        compiler_params=_params(("parallel", "arbitrary")),
        name="merge_gates",
    )(ya, ob, p_a, p_b, z2, z2)


def _resid_matmul_kernel(a_ref, w_ref, x_ref, o_ref):
    o_ref[...] = x_ref[...] + jnp.dot(a_ref[...].astype(BF16), w_ref[...], preferred_element_type=F32)


def _resid_matmul(a, w, x, *, tm, tn):
    m, k = a.shape
    n = w.shape[1]
    return pl.pallas_call(
        _resid_matmul_kernel,
        grid=(m // tm, n // tn),
        in_specs=[pl.BlockSpec((tm, k), lambda i, j: (i, 0)),
                  pl.BlockSpec((k, tn), lambda i, j: (0, j)),
                  pl.BlockSpec((tm, tn), lambda i, j: (i, j))],
        out_specs=pl.BlockSpec((tm, tn), lambda i, j: (i, j)),
        out_shape=jax.ShapeDtypeStruct((m, n), F32),
        compiler_params=_params(("parallel", "arbitrary")),
        name="out_proj_residual",
    )(a, w, x)


def _ff2_kernel(f_ref, w_ref, h_ref, g_ref, o_ref, acc_ref):
    kk = pl.program_id(1)

    @pl.when(kk == 0)
    def _():
        acc_ref[...] = h_ref[...]

    acc_ref[...] += jnp.dot(f_ref[...].astype(BF16), w_ref[...], preferred_element_type=F32)

    @pl.when(kk == pl.num_programs(1) - 1)
    def _():
        y = acc_ref[...]
        ms = jnp.mean(y * y, axis=-1, keepdims=True)
        o_ref[...] = y * lax.rsqrt(ms + RMS_EPS) * g_ref[...]


def _ff2(f, w, h, g, *, tm, tk):
    m, k = f.shape
    n = w.shape[1]
    return pl.pallas_call(
        _ff2_kernel,
        grid=(m // tm, k // tk),
        in_specs=[pl.BlockSpec((tm, tk), lambda i, kk: (i, kk)),
                  pl.BlockSpec((tk, n), lambda i, kk: (kk, 0)),
                  pl.BlockSpec((tm, n), lambda i, kk: (i, 0)),
                  pl.BlockSpec((1, n), lambda i, kk: (0, 0))],
        out_specs=pl.BlockSpec((tm, n), lambda i, kk: (i, 0)),
        out_shape=jax.ShapeDtypeStruct((m, n), F32),
        scratch_shapes=[pltpu.VMEM((tm, n), F32)],
        compiler_params=_params(("parallel", "arbitrary")),
        name="ff2_residual_norm",
    )(f, w, h, g)


def _pad_cols(w, width):
    return jnp.pad(w, ((0, 0), (0, width - w.shape[1])))


def _pad_rows(w, height):
    return jnp.pad(w, ((0, height - w.shape[0]), (0, 0)))


def kernel(x, norm_mix_g, w_in, mu_shift, w0, w_up, a0, a_up, g_up, k_k, k_a, r_k, lnx_g, lnx_b, lambda_q1, lambda_k1, lambda_q2, lambda_k2, subln_g, rel_bias, p_a, p_b, w_out, norm_mlp_g, w_ff1, w_ff2, norm_final_g):
    b, s, d = x.shape
    depth = w_in.shape[0]
    a_width = w0.shape[1]
    d_lora, a_lora, g_lora = w_up.shape[1], a_up.shape[1], g_up.shape[1]
    n_bheads = rel_bias.shape[1] // 2
    b_width = n_bheads * LANES
    assert d_lora <= LANES and a_lora <= LANES and g_lora == 2 * LANES
    assert a_width % LANES == 0 and s % 256 == 0 and subln_g.shape[1] == LANES
    m = b * s
    tm = 512
    tn = 512

    o_wd = 3 * a_width
    o_ad = o_wd + d_lora
    o_gd = o_ad + a_lora
    o_b = o_gd + g_lora
    o_g = o_b + 2 * b_width + b_width
    lora_off = 3 * a_width
    bq_off = lora_off + 4 * LANES
    gate_off = bq_off + 3 * b_width

    def regroup_cols(w):
        return jnp.concatenate([
            w[:, :o_wd], _pad_cols(w[:, o_wd:o_ad], LANES), _pad_cols(w[:, o_ad:o_gd], LANES),
            w[:, o_gd:o_b], w[:, o_b:]], axis=1)

    h = x.reshape(m, d)
    for l in range(depth):
        lambda_init = 0.8 - 0.6 * math.exp(-0.3 * l)
        w_in_r = regroup_cols(w_in[l]).astype(BF16)
        mu = mu_shift[l][None, :]
        mu_l = regroup_cols(jnp.pad(mu, ((0, 0), (0, w_in.shape[2] - mu.shape[1]))))[:, lora_off:bq_off]

        z2 = _norm_matmul(h, norm_mix_g[l][None, :], w_in_r, tm=tm, tn=tn, relu2=False, out_dtype=F32)
        z3 = z2.reshape(b, s, z2.shape[1])

        ya = _rwkv_branch(
            z3, mu[:, :a_width], mu[:, a_width:2 * a_width], mu[:, 2 * a_width:3 * a_width], mu_l,
            w0[l][None, :], _pad_rows(w_up[l], LANES).astype(BF16), a0[l][None, :],
            _pad_rows(a_up[l], LANES).astype(BF16), g_up[l].astype(BF16),
            k_k[l][None, :], k_a[l][None, :], r_k[l].reshape(1, a_width), lnx_g[l][None, :], lnx_b[l][None, :],
            a_width=a_width, lora_block=lora_off // (4 * LANES), ts=256, out_dtype=F32)

        ob = _attn_branch(
            z3, _bias_tiles(rel_bias, n_bheads), lambda_q1[l][None, :], lambda_k1[l][None, :],
            lambda_q2[l][None, :], lambda_k2[l][None, :], subln_g[l][None, :],
            q_block=bq_off // LANES, n_heads=n_bheads, lambda_init=lambda_init, out_dtype=F32)

        merged = _merge(ya.reshape(m, a_width), ob.reshape(m, b_width), p_a[l].astype(BF16),
                        p_b[l].astype(BF16), z2, ga_block=gate_off // tn, gb_block=(gate_off + d) // tn,
                        tm=tm, tn=tn, out_dtype=F32)
        h = _resid_matmul(merged, w_out[l].astype(BF16), h, tm=tm, tn=tn)

        f = _norm_matmul(h, norm_mlp_g[l][None, :], w_ff1[l].astype(BF16), tm=tm, tn=tn, relu2=True,
                         out_dtype=BF16)
        if l + 1 < depth:
            raise NotImplementedError("final norm is fused into the last layer's MLP kernel")
        h = _ff2(f, w_ff2[l].astype(BF16), h, norm_final_g[None, :], tm=tm, tk=1024)
    return h.reshape(b, s, d)
```

```python
import functools
import math

import jax
import jax.numpy as jnp
from jax import lax
from jax.experimental import pallas as pl
from jax.experimental.pallas import tpu as pltpu

F32 = jnp.float32
BF16 = jnp.bfloat16

LANES = 128
HEAD = 64
CHUNK = 64
Q_BLOCK = 128
RMS_EPS = 1e-6
GN_EPS = 64e-5
SUBLN_EPS = 1e-5
N_BUCKETS = 32
MAX_DISTANCE = 128
NEG_INF = -1e30
VMEM_LIMIT = 56 * 1024 * 1024


def _mm(a, b):
    return jnp.dot(a.astype(BF16), b.astype(BF16), preferred_element_type=F32)


def _mm_nt(a, b):
    return lax.dot_general(a.astype(BF16), b.astype(BF16), (((1,), (1,)), ((), ())),
                           preferred_element_type=F32)


def _params(sem):
    return pltpu.CompilerParams(dimension_semantics=sem, vmem_limit_bytes=VMEM_LIMIT)


def _norm_matmul_kernel(x_ref, g_ref, w_ref, o_ref, u_ref, *, relu2):
    @pl.when(pl.program_id(1) == 0)
    def _():
        x = x_ref[...]
        ms = jnp.mean(x * x, axis=-1, keepdims=True)
        u_ref[...] = (x * lax.rsqrt(ms + RMS_EPS) * g_ref[...]).astype(BF16)

    acc = jnp.dot(u_ref[...], w_ref[...], preferred_element_type=F32)
    if relu2:
        acc = jnp.square(jnp.maximum(acc, 0.0))
    o_ref[...] = acc.astype(o_ref.dtype)


def _norm_matmul(x, g, w, *, tm, tn, relu2, out_dtype):
    m, d = x.shape
    n = w.shape[1]
    return pl.pallas_call(
        functools.partial(_norm_matmul_kernel, relu2=relu2),
        grid=(m // tm, n // tn),
        in_specs=[pl.BlockSpec((tm, d), lambda i, j: (i, 0)),
                  pl.BlockSpec((1, d), lambda i, j: (0, 0)),
                  pl.BlockSpec((d, tn), lambda i, j: (0, j))],
        out_specs=pl.BlockSpec((tm, tn), lambda i, j: (i, j)),
        out_shape=jax.ShapeDtypeStruct((m, n), out_dtype),
        scratch_shapes=[pltpu.VMEM((tm, d), BF16)],
        compiler_params=_params(("parallel", "arbitrary")),
        name="norm_matmul_relu2" if relu2 else "norm_matmul",
    )(x, g, w)


def _expand(t, lane_lo):
    zero = jnp.zeros_like(t)
    return jnp.concatenate([jnp.where(lane_lo, t, zero), jnp.where(lane_lo, zero, t)], axis=0)


def _sum_heads(x, head_ones):
    hi = x.astype(BF16)
    lo = (x - hi.astype(F32)).astype(BF16)
    return (jnp.dot(hi, head_ones, preferred_element_type=F32)
            + jnp.dot(lo, head_ones, preferred_element_type=F32))


def _rwkv_kernel(zr_ref, zk_ref, zv_ref, zl_ref, mur_ref, muk_ref, muv_ref, mul_ref,
                 w0_ref, wup_ref, a0_ref, aup_ref, gup_ref, kk_ref, ka_ref, rk_ref, lng_ref, lnb_ref,
                 o_ref, cr_ref, ck_ref, cv_ref, cl_ref, h_ref, *, n_chunks):
    c = CHUNK
    c2 = 2 * c
    ts = n_chunks * c
    chunks = range(n_chunks)

    @pl.when(pl.program_id(2) == 0)
    def _():
        cr_ref[...] = jnp.zeros_like(cr_ref)
        ck_ref[...] = jnp.zeros_like(ck_ref)
        cv_ref[...] = jnp.zeros_like(cv_ref)
        cl_ref[...] = jnp.zeros_like(cl_ref)
        h_ref[...] = jnp.zeros_like(h_ref)

    def shifted(z, carry_ref, mu):
        rows = lax.broadcasted_iota(jnp.int32, z.shape, 0)
        prev = jnp.where(rows == 0, carry_ref[...], pltpu.roll(z, 1, 0))
        carry_ref[...] = z[z.shape[0] - 1:, :]
        return z + (prev - z) * mu

    ri = lax.broadcasted_iota(jnp.int32, (c2, c2), 0)
    ci = lax.broadcasted_iota(jnp.int32, (c2, c2), 1)
    strict = ci < ri
    incl = ci <= ri
    eye = (ci == ri).astype(F32)
    head_ones = ((ri // HEAD) == (ci // HEAD)).astype(BF16)
    ti = lax.broadcasted_iota(jnp.int32, (ts, ts), 0)
    tj = lax.broadcasted_iota(jnp.int32, (ts, ts), 1)
    tri = ((tj <= ti) & ((tj // c) == (ti // c))).astype(BF16)
    lane_lo = lax.broadcasted_iota(jnp.int32, (c, LANES), 1) < HEAD

    r = shifted(zr_ref[0], cr_ref, mur_ref[...])
    k = shifted(zk_ref[0], ck_ref, muk_ref[...])
    v = shifted(zv_ref[0], cv_ref, muv_ref[...])
    lo = shifted(zl_ref[0], cl_ref, mul_ref[...])

    t_w = w0_ref[...] + _mm(jnp.tanh(lo[:, 0:LANES]), wup_ref[...])
    lw = -math.exp(-0.5) * jax.nn.sigmoid(t_w)
    a_sig = jax.nn.sigmoid(a0_ref[...] + _mm(lo[:, LANES:2 * LANES], aup_ref[...]))
    gate = _mm(jax.nn.sigmoid(lo[:, 2 * LANES:]), gup_ref[...])
    kk = k * kk_ref[...]
    kk = kk * lax.rsqrt(jnp.maximum(_sum_heads(kk * kk, head_ones), 1e-24))
    k2 = k * (1.0 + (a_sig - 1.0) * ka_ref[...])
    b_v = kk * a_sig
    bonus = _sum_heads(r * k2 * rk_ref[...], head_ones) * v
    lw_hi = lw.astype(BF16)
    lw_lo = (lw - lw_hi.astype(F32)).astype(BF16)
    cum = (jnp.dot(tri, lw_hi, preferred_element_type=F32)
           + jnp.dot(tri, lw_lo, preferred_element_type=F32))
    g_in = jnp.exp(cum)
    g_inv = jnp.exp(-cum)
    rt_all = r * g_in
    at_all = -kk * jnp.exp(cum - lw)
    kt_all = k2 * g_inv
    bt_all = b_v * g_inv

    sls = [slice(ch * c, (ch + 1) * c) for ch in chunks]
    cum_end = [cum[(ch + 1) * c - 1:(ch + 1) * c, :] for ch in chunks]
    g_tail = [jnp.exp(cum_end[ch] - cum[sls[ch]]) for ch in chunks]
    rt = [_expand(rt_all[sls[ch]], lane_lo) for ch in chunks]
    at = [_expand(at_all[sls[ch]], lane_lo) for ch in chunks]
    kt = [_expand(kt_all[sls[ch]], lane_lo) for ch in chunks]
    bt = [_expand(bt_all[sls[ch]], lane_lo) for ch in chunks]
    kb = [_expand(k2[sls[ch]] * g_tail[ch], lane_lo) for ch in chunks]
    bb = [_expand(b_v[sls[ch]] * g_tail[ch], lane_lo) for ch in chunks]
    ve = [_expand(v[sls[ch]], lane_lo) for ch in chunks]

    sc = [_mm_nt(jnp.concatenate([at[ch], rt[ch]], axis=0), jnp.concatenate([bt[ch], kt[ch]], axis=0))
          for ch in chunks]
    n_ab = [jnp.where(strict, sc[ch][:c2, :c2], 0.0) for ch in chunks]
    a_ak = [jnp.where(strict, sc[ch][:c2, c2:], 0.0) for ch in chunks]
    a_rb = [jnp.where(incl, sc[ch][c2:, :c2], 0.0) for ch in chunks]
    a_rk = [jnp.where(incl, sc[ch][c2:, c2:], 0.0) for ch in chunks]

    p = [eye + n_ab[ch] for ch in chunks]
    nk = [_mm(n_ab[ch], n_ab[ch]) for ch in chunks]
    akv = [_mm(a_ak[ch], ve[ch]) for ch in chunks]
    arkv = [_mm(a_rk[ch], ve[ch]) for ch in chunks]
    kbv = [_mm(kb[ch].T, ve[ch]) for ch in chunks]
    power = 2
    while power < c:
        if 2 * power < c:
            both = [_mm(jnp.concatenate([nk[ch], p[ch]], axis=0), nk[ch]) for ch in chunks]
            nk = [both[ch][:c2] for ch in chunks]
            p = [p[ch] + both[ch][c2:] for ch in chunks]
        else:
            p = [p[ch] + _mm(p[ch], nk[ch]) for ch in chunks]
        power *= 2

    ta = [_mm(p[ch], jnp.concatenate([at[ch], akv[ch]], axis=1)) for ch in chunks]
    rb = [_mm(a_rb[ch], ta[ch]) for ch in chunks]
    mg = [_mm(bb[ch].T, ta[ch]) for ch in chunks]
    r_hat = [rt[ch] + rb[ch][:, :LANES] for ch in chunks]
    y_in = [rb[ch][:, LANES:] + arkv[ch] for ch in chunks]
    m_state = [eye * jnp.exp(cum_end[ch]) + mg[ch][:, :LANES] for ch in chunks]
    g_state = [mg[ch][:, LANES:] + kbv[ch] for ch in chunks]

    h = h_ref[...]
    ys = []
    for ch in chunks:
        ye = _mm(r_hat[ch], h) + y_in[ch]
        h = _mm(m_state[ch], h) + g_state[ch]
        ys.append(ye[:c] + ye[c:])
    h_ref[...] = h
    y = jnp.concatenate(ys, axis=0)

    mean = _sum_heads(y, head_ones) * (1.0 / HEAD)
    d = y - mean
    var = _sum_heads(d * d, head_ones) * (1.0 / HEAD)
    yn = d * lax.rsqrt(var + GN_EPS) * lng_ref[...] + lnb_ref[...]
    o_ref[0] = ((yn + bonus) * gate).astype(o_ref.dtype)


def _rwkv_branch(z3, mu_r, mu_k, mu_v, mu_l, w0, wup, a0, aup, gup, k_k, k_a, r_k, lnx_g, lnx_b,
                 *, a_width, lora_block, ts, out_dtype):
    b, s, _ = z3.shape
    n_pairs = a_width // LANES
    wb = a_width // LANES
    lw = 4 * LANES

    def col(off):
        return lambda bi, p, si: (bi, si, off + p)

    def vec():
        return pl.BlockSpec((1, LANES), lambda bi, p, si: (0, p))

    in_specs = [
        pl.BlockSpec((1, ts, LANES), col(0)),
        pl.BlockSpec((1, ts, LANES), col(wb)),
        pl.BlockSpec((1, ts, LANES), col(2 * wb)),
        pl.BlockSpec((1, ts, lw), lambda bi, p, si: (bi, si, lora_block)),
        vec(), vec(), vec(),
        pl.BlockSpec((1, lw), lambda bi, p, si: (0, 0)),
        vec(),
        pl.BlockSpec((LANES, LANES), lambda bi, p, si: (0, p)),
        vec(),
        pl.BlockSpec((LANES, LANES), lambda bi, p, si: (0, p)),
        pl.BlockSpec((2 * LANES, LANES), lambda bi, p, si: (0, p)),
        vec(), vec(), vec(), vec(), vec(),
    ]
    return pl.pallas_call(
        functools.partial(_rwkv_kernel, n_chunks=ts // CHUNK),
        grid=(b, n_pairs, s // ts),
        in_specs=in_specs,
        out_specs=pl.BlockSpec((1, ts, LANES), lambda bi, p, si: (bi, si, p)),
        out_shape=jax.ShapeDtypeStruct((b, s, a_width), out_dtype),
        scratch_shapes=[pltpu.VMEM((1, LANES), F32), pltpu.VMEM((1, LANES), F32),
                        pltpu.VMEM((1, LANES), F32), pltpu.VMEM((1, lw), F32),
                        pltpu.VMEM((LANES, LANES), F32)],
        compiler_params=_params(("parallel", "parallel", "arbitrary")),
        name="rwkv7_chunked",
    )(z3, z3, z3, z3, mu_r, mu_k, mu_v, mu_l, w0, wup, a0, aup, gup, k_k, k_a, r_k, lnx_g, lnx_b)


def _attn_kernel(q_ref, k_ref, v_ref, bias_ref, lq1_ref, lk1_ref, lq2_ref, lk2_ref, sg_ref, o_ref,
                 kb_ref, vb_ref, s_ref, p_ref, *, n_qblocks, lambda_init):
    qr = Q_BLOCK
    kb_ref[...] = k_ref[0].astype(BF16)
    vb_ref[...] = v_ref[0].astype(BF16)
    lam = (jnp.exp(jnp.sum(lq1_ref[...] * lk1_ref[...])) - jnp.exp(jnp.sum(lq2_ref[...] * lk2_ref[...]))
           + lambda_init)
    lane_lo = lax.broadcasted_iota(jnp.int32, (qr, LANES), 1) < HEAD
    scale = HEAD ** -0.5

    def key_tiles(qb):
        return [(j, min(2, qb + 1 - j)) for j in range(0, qb + 1, 2)]

    def logits(qb):
        q = q_ref[0, qb * qr:(qb + 1) * qr, :] * scale
        qe = _expand(q, lane_lo).astype(BF16)
        m_l = None
        for j0, nb in key_tiles(qb):
            s = _mm_nt(qe, kb_ref[j0 * qr:(j0 + nb) * qr, :])
            for t in range(nb):
                j = j0 + t
                sj = s[:, t * qr:(t + 1) * qr] + bias_ref[0, min(qb - j, 2)]
                s_ref[qb % 2, :, j * qr:(j + 1) * qr] = sj
                m_l = sj if m_l is None else jnp.maximum(m_l, sj)
        return m_l

    def weighted_values(qb, m_l):
        m = jnp.max(m_l, axis=-1, keepdims=True)
        l_l = jnp.zeros((2 * qr, qr), F32)
        for j in range(qb + 1):
            pj = jnp.exp(s_ref[qb % 2, :, j * qr:(j + 1) * qr] - m)
            l_l = l_l + pj
            p_ref[qb % 2, :, j * qr:(j + 1) * qr] = pj.astype(BF16)
        n_keys = (qb + 1) * qr
        acc = jnp.dot(p_ref[qb % 2, :, :n_keys], vb_ref[:n_keys, :], preferred_element_type=F32)
        o = acc / jnp.sum(l_l, axis=-1, keepdims=True)
        o = o[:qr] - lam * o[qr:]
        o = o * lax.rsqrt(jnp.mean(o * o, axis=-1, keepdims=True) + SUBLN_EPS) * sg_ref[...]
        o = o * (1.0 - lambda_init)
        o_ref[0, qb * qr:(qb + 1) * qr, :] = o.astype(o_ref.dtype)

    pending = logits(0)
    for qb in range(n_qblocks):
        nxt = logits(qb + 1) if qb + 1 < n_qblocks else None
        weighted_values(qb, pending)
        pending = nxt


def _attn_branch(z3, bias, lq1, lk1, lq2, lk2, subln_g, *, q_block, n_heads, lambda_init, out_dtype):
    b, s, _ = z3.shape

    def col(off):
        return lambda bi, hh: (bi, 0, off + hh)

    def small(n):
        return pl.BlockSpec((1, n), lambda bi, hh: (0, 0))

    return pl.pallas_call(
        functools.partial(_attn_kernel, n_qblocks=s // Q_BLOCK, lambda_init=lambda_init),
        grid=(b, n_heads),
        in_specs=[pl.BlockSpec((1, s, LANES), col(q_block)),
                  pl.BlockSpec((1, s, LANES), col(q_block + n_heads)),
                  pl.BlockSpec((1, s, LANES), col(q_block + 2 * n_heads)),
                  pl.BlockSpec((1, 3, 2 * Q_BLOCK, Q_BLOCK), lambda bi, hh: (hh, 0, 0, 0)),
                  small(HEAD), small(HEAD), small(HEAD), small(HEAD), small(LANES)],
        out_specs=pl.BlockSpec((1, s, LANES), lambda bi, hh: (bi, 0, hh)),
        out_shape=jax.ShapeDtypeStruct((b, s, n_heads * LANES), out_dtype),
        scratch_shapes=[pltpu.VMEM((s, LANES), BF16), pltpu.VMEM((s, LANES), BF16),
                        pltpu.VMEM((2, 2 * Q_BLOCK, s), F32), pltpu.VMEM((2, 2 * Q_BLOCK, s), BF16)],
        compiler_params=_params(("parallel", "parallel")),
        name="diff_attention",
    )(z3, z3, z3, bias, lq1, lk1, lq2, lk2, subln_g)


def _t5_bucket(rel):
    nb = N_BUCKETS // 2
    max_exact = nb // 2
    ret = jnp.where(rel > 0, nb, 0)
    n = jnp.abs(rel)
    nf = jnp.maximum(n, 1).astype(jnp.float32)
    large = max_exact + (jnp.log(nf / max_exact) / math.log(MAX_DISTANCE / max_exact)
                         * (nb - max_exact)).astype(jnp.int32)
    large = jnp.minimum(large, nb - 1)
    return ret + jnp.where(n < max_exact, n, large)


def _bias_tiles(rel_bias, n_heads):
    qi = jnp.arange(Q_BLOCK)[:, None]
    kj = jnp.arange(Q_BLOCK)[None, :]
    table = rel_bias.astype(F32)
    tiles = []
    for dist in range(3):
        rel = kj - qi - dist * Q_BLOCK
        t = jnp.transpose(table[_t5_bucket(rel)], (2, 0, 1))
        if dist == 0:
            t = jnp.where(((kj // CHUNK) <= (qi // CHUNK))[None], t, NEG_INF)
        tiles.append(t.reshape(n_heads, 2 * Q_BLOCK, Q_BLOCK))
    return jnp.stack(tiles, axis=1)


def _merge_kernel(ya_ref, ob_ref, pa_ref, pb_ref, ga_ref, gb_ref, o_ref):
    oa = jnp.dot(ya_ref[...].astype(BF16), pa_ref[...], preferred_element_type=F32)
    ob = jnp.dot(ob_ref[...].astype(BF16), pb_ref[...], preferred_element_type=F32)
    o_ref[...] = (jax.nn.sigmoid(ga_ref[...]) * oa + jax.nn.sigmoid(gb_ref[...]) * ob).astype(o_ref.dtype)


def _merge(ya, ob, p_a, p_b, z2, *, ga_block, gb_block, tm, tn, out_dtype):
    m, ka = ya.shape
    kb = ob.shape[1]
    n = p_a.shape[1]
    return pl.pallas_call(
        _merge_kernel,
        grid=(m // tm, n // tn),
        in_specs=[pl.BlockSpec((tm, ka), lambda i, j: (i, 0)),
                  pl.BlockSpec((tm, kb), lambda i, j: (i, 0)),
                  pl.BlockSpec((ka, tn), lambda i, j: (0, j)),
                  pl.BlockSpec((kb, tn), lambda i, j: (0, j)),
                  pl.BlockSpec((tm, tn), lambda i, j: (i, ga_block + j)),
                  pl.BlockSpec((tm, tn), lambda i, j: (i, gb_block + j))],
        out_specs=pl.BlockSpec((tm, tn), lambda i, j: (i, j)),
        out_shape=jax.ShapeDtypeStruct((m, n), out_dtype),
        compiler_params=_params(("parallel", "arbitrary")),
        name="merge_gates",
    )(ya, ob, p_a, p_b, z2, z2)


def _resid_matmul_kernel(a_ref, w_ref, x_ref, o_ref):
    o_ref[...] = x_ref[...] + jnp.dot(a_ref[...].astype(BF16), w_ref[...], preferred_element_type=F32)


def _resid_matmul(a, w, x, *, tm, tn):
    m, k = a.shape
    n = w.shape[1]
    return pl.pallas_call(
        _resid_matmul_kernel,
        grid=(m // tm, n // tn),
        in_specs=[pl.BlockSpec((tm, k), lambda i, j: (i, 0)),
                  pl.BlockSpec((k, tn), lambda i, j: (0, j)),
                  pl.BlockSpec((tm, tn), lambda i, j: (i, j))],
        out_specs=pl.BlockSpec((tm, tn), lambda i, j: (i, j)),
        out_shape=jax.ShapeDtypeStruct((m, n), F32),
        compiler_params=_params(("parallel", "arbitrary")),
        name="out_proj_residual",
    )(a, w, x)


def _ff2_kernel(f_ref, w_ref, h_ref, g_ref, o_ref, acc_ref):
    kk = pl.program_id(1)

    @pl.when(kk == 0)
    def _():
        acc_ref[...] = h_ref[...]

    acc_ref[...] += jnp.dot(f_ref[...].astype(BF16), w_ref[...], preferred_element_type=F32)

    @pl.when(kk == pl.num_programs(1) - 1)
    def _():
        y = acc_ref[...]
        ms = jnp.mean(y * y, axis=-1, keepdims=True)
        o_ref[...] = y * lax.rsqrt(ms + RMS_EPS) * g_ref[...]


def _ff2(f, w, h, g, *, tm, tk):
    m, k = f.shape
    n = w.shape[1]
    return pl.pallas_call(
        _ff2_kernel,
        grid=(m // tm, k // tk),
        in_specs=[pl.BlockSpec((tm, tk), lambda i, kk: (i, kk)),
                  pl.BlockSpec((tk, n), lambda i, kk: (kk, 0)),
                  pl.BlockSpec((tm, n), lambda i, kk: (i, 0)),
                  pl.BlockSpec((1, n), lambda i, kk: (0, 0))],
        out_specs=pl.BlockSpec((tm, n), lambda i, kk: (i, 0)),
        out_shape=jax.ShapeDtypeStruct((m, n), F32),
        scratch_shapes=[pltpu.VMEM((tm, n), F32)],
        compiler_params=_params(("parallel", "arbitrary")),
        name="ff2_residual_norm",
    )(f, w, h, g)


def _pad_cols(w, width):
    return jnp.pad(w, ((0, 0), (0, width - w.shape[1])))


def _pad_rows(w, height):
    return jnp.pad(w, ((0, height - w.shape[0]), (0, 0)))


def kernel(x, norm_mix_g, w_in, mu_shift, w0, w_up, a0, a_up, g_up, k_k, k_a, r_k, lnx_g, lnx_b, lambda_q1, lambda_k1, lambda_q2, lambda_k2, subln_g, rel_bias, p_a, p_b, w_out, norm_mlp_g, w_ff1, w_ff2, norm_final_g):
    b, s, d = x.shape
    depth = w_in.shape[0]
    a_width = w0.shape[1]
    d_lora, a_lora, g_lora = w_up.shape[1], a_up.shape[1], g_up.shape[1]
    n_bheads = rel_bias.shape[1] // 2
    b_width = n_bheads * LANES
    assert depth == 1, "the final norm is fused into the (single) layer's MLP kernel"
    assert d_lora <= LANES and a_lora <= LANES and g_lora == 2 * LANES
    assert a_width % LANES == 0 and s % 512 == 0 and subln_g.shape[1] == LANES
    m = b * s
    tm = 512
    tn = 512

    o_wd = 3 * a_width
    o_ad = o_wd + d_lora
    o_gd = o_ad + a_lora
    o_b = o_gd + g_lora
    lora_off = 3 * a_width
    bq_off = lora_off + 4 * LANES
    gate_off = bq_off + 3 * b_width

    def regroup_cols(w):
        return jnp.concatenate([
            w[:, :o_wd], _pad_cols(w[:, o_wd:o_ad], LANES), _pad_cols(w[:, o_ad:o_gd], LANES),
            w[:, o_gd:o_b], w[:, o_b:]], axis=1)

    h = x.reshape(m, d)
    l = 0
    lambda_init = 0.8 - 0.6 * math.exp(-0.3 * l)
    w_in_r = regroup_cols(w_in[l]).astype(BF16)
    mu = mu_shift[l][None, :]
    mu_l = regroup_cols(jnp.pad(mu, ((0, 0), (0, w_in.shape[2] - mu.shape[1]))))[:, lora_off:bq_off]

    z2 = _norm_matmul(h, norm_mix_g[l][None, :], w_in_r, tm=tm, tn=tn, relu2=False, out_dtype=F32)
    z3 = z2.reshape(b, s, z2.shape[1])

    ya = _rwkv_branch(
        z3, mu[:, :a_width], mu[:, a_width:2 * a_width], mu[:, 2 * a_width:3 * a_width], mu_l,
        w0[l][None, :], _pad_rows(w_up[l], LANES).astype(BF16), a0[l][None, :],
        _pad_rows(a_up[l], LANES).astype(BF16), g_up[l].astype(BF16),
        k_k[l][None, :], k_a[l][None, :], r_k[l].reshape(1, a_width), lnx_g[l][None, :], lnx_b[l][None, :],
        a_width=a_width, lora_block=lora_off // (4 * LANES), ts=512, out_dtype=F32)

    ob = _attn_branch(
        z3, _bias_tiles(rel_bias, n_bheads), lambda_q1[l][None, :], lambda_k1[l][None, :],
        lambda_q2[l][None, :], lambda_k2[l][None, :], subln_g[l][None, :],
        q_block=bq_off // LANES, n_heads=n_bheads, lambda_init=lambda_init, out_dtype=F32)

    merged = _merge(ya.reshape(m, a_width), ob.reshape(m, b_width), p_a[l].astype(BF16),
                    p_b[l].astype(BF16), z2, ga_block=gate_off // tn, gb_block=(gate_off + d) // tn,
                    tm=tm, tn=tn, out_dtype=F32)
    h = _resid_matmul(merged, w_out[l].astype(BF16), h, tm=tm, tn=tn)

    f = _norm_matmul(h, norm_mlp_g[l][None, :], w_ff1[l].astype(BF16), tm=tm, tn=tn, relu2=True,
                     out_dtype=BF16)
    h = _ff2(f, w_ff2[l].astype(BF16), h, norm_final_g[None, :], tm=tm, tk=1024)
    return h.reshape(b, s, d)
```

```python
import functools
import math

import jax
import jax.numpy as jnp
from jax import lax
from jax.experimental import pallas as pl
from jax.experimental.pallas import tpu as pltpu

F32 = jnp.float32
BF16 = jnp.bfloat16

LANES = 128
HEAD = 64
CHUNK = 64
Q_BLOCK = 128
RMS_EPS = 1e-6
GN_EPS = 64e-5
SUBLN_EPS = 1e-5
N_BUCKETS = 32
MAX_DISTANCE = 128
NEG_INF = -1e30
VMEM_LIMIT = 56 * 1024 * 1024


def _mm(a, b):
    return jnp.dot(a.astype(BF16), b.astype(BF16), preferred_element_type=F32)


def _mm_nt(a, b):
    return lax.dot_general(a.astype(BF16), b.astype(BF16), (((1,), (1,)), ((), ())),
                           preferred_element_type=F32)


def _params(sem):
    return pltpu.CompilerParams(dimension_semantics=sem, vmem_limit_bytes=VMEM_LIMIT)


def _norm_matmul_kernel(x_ref, g_ref, w_ref, o_ref, u_ref, *, relu2):
    @pl.when(pl.program_id(1) == 0)
    def _():
        x = x_ref[...]
        ms = jnp.mean(x * x, axis=-1, keepdims=True)
        u_ref[...] = (x * lax.rsqrt(ms + RMS_EPS) * g_ref[...]).astype(BF16)

    acc = jnp.dot(u_ref[...], w_ref[...], preferred_element_type=F32)
    if relu2:
        acc = jnp.square(jnp.maximum(acc, 0.0))
    o_ref[...] = acc.astype(o_ref.dtype)


def _norm_matmul(x, g, w, *, tm, tn, relu2, out_dtype):
    m, d = x.shape
    n = w.shape[1]
    return pl.pallas_call(
        functools.partial(_norm_matmul_kernel, relu2=relu2),
        grid=(m // tm, n // tn),
        in_specs=[pl.BlockSpec((tm, d), lambda i, j: (i, 0)),
                  pl.BlockSpec((1, d), lambda i, j: (0, 0)),
                  pl.BlockSpec((d, tn), lambda i, j: (0, j))],
        out_specs=pl.BlockSpec((tm, tn), lambda i, j: (i, j)),
        out_shape=jax.ShapeDtypeStruct((m, n), out_dtype),
        scratch_shapes=[pltpu.VMEM((tm, d), BF16)],
        compiler_params=_params(("parallel", "arbitrary")),
        name="norm_matmul_relu2" if relu2 else "norm_matmul",
    )(x, g, w)


def _expand(t, lane_lo):
    zero = jnp.zeros_like(t)
    return jnp.concatenate([jnp.where(lane_lo, t, zero), jnp.where(lane_lo, zero, t)], axis=0)


def _sum_heads(x, head_ones):
    hi = x.astype(BF16)
    lo = (x - hi.astype(F32)).astype(BF16)
    return (jnp.dot(hi, head_ones, preferred_element_type=F32)
            + jnp.dot(lo, head_ones, preferred_element_type=F32))


def _rwkv_kernel(zr_ref, zk_ref, zv_ref, zl_ref, mur_ref, muk_ref, muv_ref, mul_ref,
                 w0_ref, wup_ref, a0_ref, aup_ref, gup_ref, kk_ref, ka_ref, rk_ref, lng_ref, lnb_ref,
                 o_ref, cr_ref, ck_ref, cv_ref, cl_ref, h_ref, *, n_chunks):
    c = CHUNK
    c2 = 2 * c
    ts = n_chunks * c
    chunks = range(n_chunks)

    @pl.when(pl.program_id(2) == 0)
    def _():
        cr_ref[...] = jnp.zeros_like(cr_ref)
        ck_ref[...] = jnp.zeros_like(ck_ref)
        cv_ref[...] = jnp.zeros_like(cv_ref)
        cl_ref[...] = jnp.zeros_like(cl_ref)
        h_ref[...] = jnp.zeros_like(h_ref)

    def shifted(z, carry_ref, mu):
        rows = lax.broadcasted_iota(jnp.int32, z.shape, 0)
        prev = jnp.where(rows == 0, carry_ref[...], pltpu.roll(z, 1, 0))
        carry_ref[...] = z[z.shape[0] - 1:, :]
        return z + (prev - z) * mu

    ri = lax.broadcasted_iota(jnp.int32, (c2, c2), 0)
    ci = lax.broadcasted_iota(jnp.int32, (c2, c2), 1)
    strict = ci < ri
    incl = ci <= ri
    eye = (ci == ri).astype(F32)
    head_ones = ((ri // HEAD) == (ci // HEAD)).astype(BF16)
    ti = lax.broadcasted_iota(jnp.int32, (ts, ts), 0)
    tj = lax.broadcasted_iota(jnp.int32, (ts, ts), 1)
    tri = ((tj <= ti) & ((tj // c) == (ti // c))).astype(BF16)
    lane_lo = lax.broadcasted_iota(jnp.int32, (c, LANES), 1) < HEAD

    r = shifted(zr_ref[0].astype(F32), cr_ref, mur_ref[...])
    k = shifted(zk_ref[0].astype(F32), ck_ref, muk_ref[...])
    v = shifted(zv_ref[0].astype(F32), cv_ref, muv_ref[...])
    lo = shifted(zl_ref[0].astype(F32), cl_ref, mul_ref[...])

    t_w = w0_ref[...] + _mm(jnp.tanh(lo[:, 0:LANES]), wup_ref[...])
    lw = -math.exp(-0.5) * jax.nn.sigmoid(t_w)
    a_sig = jax.nn.sigmoid(a0_ref[...] + _mm(lo[:, LANES:2 * LANES], aup_ref[...]))
    gate = _mm(jax.nn.sigmoid(lo[:, 2 * LANES:]), gup_ref[...])
    kk = k * kk_ref[...]
    kk = kk * lax.rsqrt(jnp.maximum(_sum_heads(kk * kk, head_ones), 1e-24))
    k2 = k * (1.0 + (a_sig - 1.0) * ka_ref[...])
    b_v = kk * a_sig
    bonus = _sum_heads(r * k2 * rk_ref[...], head_ones) * v
    lw_hi = lw.astype(BF16)
    lw_lo = (lw - lw_hi.astype(F32)).astype(BF16)
    cum = (jnp.dot(tri, lw_hi, preferred_element_type=F32)
           + jnp.dot(tri, lw_lo, preferred_element_type=F32))
    g_in = jnp.exp(cum)
    g_inv = jnp.exp(-cum)
    rt_all = r * g_in
    at_all = -kk * jnp.exp(cum - lw)
    kt_all = k2 * g_inv
    bt_all = b_v * g_inv

    sls = [slice(ch * c, (ch + 1) * c) for ch in chunks]
    cum_end = [cum[(ch + 1) * c - 1:(ch + 1) * c, :] for ch in chunks]
    g_tail = [jnp.exp(cum_end[ch] - cum[sls[ch]]) for ch in chunks]
    rt = [_expand(rt_all[sls[ch]], lane_lo) for ch in chunks]
    at = [_expand(at_all[sls[ch]], lane_lo) for ch in chunks]
    kt = [_expand(kt_all[sls[ch]], lane_lo) for ch in chunks]
    bt = [_expand(bt_all[sls[ch]], lane_lo) for ch in chunks]
    kb = [_expand(k2[sls[ch]] * g_tail[ch], lane_lo) for ch in chunks]
    bb = [_expand(b_v[sls[ch]] * g_tail[ch], lane_lo) for ch in chunks]
    ve = [_expand(v[sls[ch]], lane_lo) for ch in chunks]

    sc = [_mm_nt(jnp.concatenate([at[ch], rt[ch]], axis=0), jnp.concatenate([bt[ch], kt[ch]], axis=0))
          for ch in chunks]
    n_ab = [jnp.where(strict, sc[ch][:c2, :c2], 0.0) for ch in chunks]
    a_ak = [jnp.where(strict, sc[ch][:c2, c2:], 0.0) for ch in chunks]
    a_rb = [jnp.where(incl, sc[ch][c2:, :c2], 0.0) for ch in chunks]
    a_rk = [jnp.where(incl, sc[ch][c2:, c2:], 0.0) for ch in chunks]

    p = [eye + n_ab[ch] for ch in chunks]
    nk = [_mm(n_ab[ch], n_ab[ch]) for ch in chunks]
    akv = [_mm(a_ak[ch], ve[ch]) for ch in chunks]
    arkv = [_mm(a_rk[ch], ve[ch]) for ch in chunks]
    kbv = [_mm(kb[ch].T, ve[ch]) for ch in chunks]
    power = 2
    while power < c:
        if 2 * power < c:
            both = [_mm(jnp.concatenate([nk[ch], p[ch]], axis=0), nk[ch]) for ch in chunks]
            nk = [both[ch][:c2] for ch in chunks]
            p = [p[ch] + both[ch][c2:] for ch in chunks]
        else:
            p = [p[ch] + _mm(p[ch], nk[ch]) for ch in chunks]
        power *= 2

    ta = [_mm(p[ch], jnp.concatenate([at[ch], akv[ch]], axis=1)) for ch in chunks]
    rb = [_mm(a_rb[ch], ta[ch]) for ch in chunks]
    mg = [_mm(bb[ch].T, ta[ch]) for ch in chunks]
    r_hat = [rt[ch] + rb[ch][:, :LANES] for ch in chunks]
    y_in = [rb[ch][:, LANES:] + arkv[ch] for ch in chunks]
    m_state = [eye * jnp.exp(cum_end[ch]) + mg[ch][:, :LANES] for ch in chunks]
    g_state = [mg[ch][:, LANES:] + kbv[ch] for ch in chunks]

    h = h_ref[...]
    ys = []
    for ch in chunks:
        ye = _mm(r_hat[ch], h) + y_in[ch]
        h = _mm(m_state[ch], h) + g_state[ch]
        ys.append(ye[:c] + ye[c:])
    h_ref[...] = h
    y = jnp.concatenate(ys, axis=0)

    mean = _sum_heads(y, head_ones) * (1.0 / HEAD)
    d = y - mean
    var = _sum_heads(d * d, head_ones) * (1.0 / HEAD)
    yn = d * lax.rsqrt(var + GN_EPS) * lng_ref[...] + lnb_ref[...]
    o_ref[0] = ((yn + bonus) * gate).astype(o_ref.dtype)


def _rwkv_branch(z3, mu_r, mu_k, mu_v, mu_l, w0, wup, a0, aup, gup, k_k, k_a, r_k, lnx_g, lnx_b,
                 *, a_width, rkv_block, lora_block, ts, out_dtype):
    b, s, _ = z3.shape
    n_pairs = a_width // LANES
    wb = a_width // LANES
    lw = 4 * LANES

    def col(off):
        return lambda bi, p, si: (bi, si, off + p)

    def vec():
        return pl.BlockSpec((1, LANES), lambda bi, p, si: (0, p))

    in_specs = [
        pl.BlockSpec((1, ts, LANES), col(rkv_block)),
        pl.BlockSpec((1, ts, LANES), col(rkv_block + wb)),
        pl.BlockSpec((1, ts, LANES), col(rkv_block + 2 * wb)),
        pl.BlockSpec((1, ts, lw), lambda bi, p, si: (bi, si, lora_block)),
        vec(), vec(), vec(),
        pl.BlockSpec((1, lw), lambda bi, p, si: (0, 0)),
        vec(),
        pl.BlockSpec((LANES, LANES), lambda bi, p, si: (0, p)),
        vec(),
        pl.BlockSpec((LANES, LANES), lambda bi, p, si: (0, p)),
        pl.BlockSpec((2 * LANES, LANES), lambda bi, p, si: (0, p)),
        vec(), vec(), vec(), vec(), vec(),
    ]
    return pl.pallas_call(
        functools.partial(_rwkv_kernel, n_chunks=ts // CHUNK),
        grid=(b, n_pairs, s // ts),
        in_specs=in_specs,
        out_specs=pl.BlockSpec((1, ts, LANES), lambda bi, p, si: (bi, si, p)),
        out_shape=jax.ShapeDtypeStruct((b, s, a_width), out_dtype),
        scratch_shapes=[pltpu.VMEM((1, LANES), F32), pltpu.VMEM((1, LANES), F32),
                        pltpu.VMEM((1, LANES), F32), pltpu.VMEM((1, lw), F32),
                        pltpu.VMEM((LANES, LANES), F32)],
        compiler_params=_params(("parallel", "parallel", "arbitrary")),
        name="rwkv7_chunked",
    )(z3, z3, z3, z3, mu_r, mu_k, mu_v, mu_l, w0, wup, a0, aup, gup, k_k, k_a, r_k, lnx_g, lnx_b)


def _attn_kernel(q_ref, k_ref, v_ref, bias_ref, lq1_ref, lk1_ref, lq2_ref, lk2_ref, sg_ref, o_ref,
                 s_ref, p_ref, *, n_qblocks, lambda_init):
    qr = Q_BLOCK
    lam = (jnp.exp(jnp.sum(lq1_ref[...] * lk1_ref[...])) - jnp.exp(jnp.sum(lq2_ref[...] * lk2_ref[...]))
           + lambda_init)
    lane_lo = lax.broadcasted_iota(jnp.int32, (qr, LANES), 1) < HEAD
    scale = HEAD ** -0.5

    def key_tiles(qb):
        return [(j, min(2, qb + 1 - j)) for j in range(0, qb + 1, 2)]

    def logits(qb):
        q = q_ref[0, qb * qr:(qb + 1) * qr, :].astype(BF16) * scale
        qe = _expand(q, lane_lo)
        m_l = None
        for j0, nb in key_tiles(qb):
            s = _mm_nt(qe, k_ref[0, j0 * qr:(j0 + nb) * qr, :])
            for t in range(nb):
                j = j0 + t
                sj = s[:, t * qr:(t + 1) * qr] + bias_ref[0, min(qb - j, 2)]
                s_ref[qb % 2, :, j * qr:(j + 1) * qr] = sj
                m_l = sj if m_l is None else jnp.maximum(m_l, sj)
        return m_l

    def weighted_values(qb, m_l):
        m = jnp.max(m_l, axis=-1, keepdims=True)
        l_l = jnp.zeros((2 * qr, qr), F32)
        for j in range(qb + 1):
            pj = jnp.exp(s_ref[qb % 2, :, j * qr:(j + 1) * qr] - m)
            l_l = l_l + pj
            p_ref[qb % 2, :, j * qr:(j + 1) * qr] = pj.astype(BF16)
        n_keys = (qb + 1) * qr
        acc = jnp.dot(p_ref[qb % 2, :, :n_keys], v_ref[0, :n_keys, :].astype(BF16),
                      preferred_element_type=F32)
        o = acc / jnp.sum(l_l, axis=-1, keepdims=True)
        o = o[:qr] - lam * o[qr:]
        o = o * lax.rsqrt(jnp.mean(o * o, axis=-1, keepdims=True) + SUBLN_EPS) * sg_ref[...]
        o = o * (1.0 - lambda_init)
        o_ref[0, qb * qr:(qb + 1) * qr, :] = o.astype(o_ref.dtype)

    pending = logits(0)
    for qb in range(n_qblocks):
        nxt = logits(qb + 1) if qb + 1 < n_qblocks else None
        weighted_values(qb, pending)
        pending = nxt


def _attn_branch(z3, bias, lq1, lk1, lq2, lk2, subln_g, *, q_block, n_heads, lambda_init, out_dtype):
    b, s, _ = z3.shape

    def col(off):
        return lambda bi, hh: (bi, 0, off + hh)

    def small(n):
        return pl.BlockSpec((1, n), lambda bi, hh: (0, 0))

    return pl.pallas_call(
        functools.partial(_attn_kernel, n_qblocks=s // Q_BLOCK, lambda_init=lambda_init),
        grid=(b, n_heads),
        in_specs=[pl.BlockSpec((1, s, LANES), col(q_block)),
                  pl.BlockSpec((1, s, LANES), col(q_block + n_heads)),
                  pl.BlockSpec((1, s, LANES), col(q_block + 2 * n_heads)),
                  pl.BlockSpec((1, 3, 2 * Q_BLOCK, Q_BLOCK), lambda bi, hh: (hh, 0, 0, 0)),
                  small(HEAD), small(HEAD), small(HEAD), small(HEAD), small(LANES)],
        out_specs=pl.BlockSpec((1, s, LANES), lambda bi, hh: (bi, 0, hh)),
        out_shape=jax.ShapeDtypeStruct((b, s, n_heads * LANES), out_dtype),
        scratch_shapes=[pltpu.VMEM((2, 2 * Q_BLOCK, s), F32), pltpu.VMEM((2, 2 * Q_BLOCK, s), BF16)],
        compiler_params=_params(("parallel", "parallel")),
        name="diff_attention",
    )(z3, z3, z3, bias, lq1, lk1, lq2, lk2, subln_g)


def _t5_bucket(rel):
    nb = N_BUCKETS // 2
    max_exact = nb // 2
    ret = jnp.where(rel > 0, nb, 0)
    n = jnp.abs(rel)
    nf = jnp.maximum(n, 1).astype(jnp.float32)
    large = max_exact + (jnp.log(nf / max_exact) / math.log(MAX_DISTANCE / max_exact)
                         * (nb - max_exact)).astype(jnp.int32)
    large = jnp.minimum(large, nb - 1)
    return ret + jnp.where(n < max_exact, n, large)


def _bias_tiles(rel_bias, n_heads):
    dist = jnp.arange(3)[:, None, None]
    qi = jnp.arange(Q_BLOCK)[None, :, None]
    kj = jnp.arange(Q_BLOCK)[None, None, :]
    bucket = _t5_bucket(kj - qi - dist * Q_BLOCK)
    onehot = (bucket[..., None] == jnp.arange(N_BUCKETS)).astype(F32)
    t = jnp.einsum("dqkn,nh->hdqk", onehot, rel_bias.astype(F32), precision=lax.Precision.HIGHEST)
    allowed = (dist > 0) | ((kj // CHUNK) <= (qi // CHUNK))
    t = jnp.where(allowed[None], t, NEG_INF)
    t = t.reshape(n_heads, 2, 3, Q_BLOCK, Q_BLOCK).transpose(0, 2, 1, 3, 4)
    return t.reshape(n_heads, 3, 2 * Q_BLOCK, Q_BLOCK)


def _merge_out_kernel(ya_ref, ob_ref, ga_ref, gb_ref, x_ref, pa_ref, pb_ref, wo_ref, o_ref):
    oa = jnp.dot(ya_ref[...], pa_ref[...], preferred_element_type=F32)
    ob = jnp.dot(ob_ref[...], pb_ref[...], preferred_element_type=F32)
    merged = (jax.nn.sigmoid(ga_ref[...].astype(F32)) * oa
              + jax.nn.sigmoid(gb_ref[...].astype(F32)) * ob)
    o_ref[...] = x_ref[...] + jnp.dot(merged.astype(BF16), wo_ref[...], preferred_element_type=F32)


def _merge_out(ya, ob, z2, x, p_a, p_b, w_out, *, gate_block, tm):
    m, ka = ya.shape
    kb = ob.shape[1]
    d = w_out.shape[1]

    def resident(shape):
        return pl.BlockSpec(shape, lambda i: (0, 0), pipeline_mode=pl.Buffered(1))

    return pl.pallas_call(
        _merge_out_kernel,
        grid=(m // tm,),
        in_specs=[pl.BlockSpec((tm, ka), lambda i: (i, 0)),
                  pl.BlockSpec((tm, kb), lambda i: (i, 0)),
                  pl.BlockSpec((tm, d), lambda i: (i, gate_block)),
                  pl.BlockSpec((tm, d), lambda i: (i, gate_block + 1)),
                  pl.BlockSpec((tm, d), lambda i: (i, 0)),
                  resident(p_a.shape), resident(p_b.shape), resident(w_out.shape)],
        out_specs=pl.BlockSpec((tm, d), lambda i: (i, 0)),
        out_shape=jax.ShapeDtypeStruct((m, d), F32),
        compiler_params=_params(("parallel",)),
        name="merge_out_proj",
    )(ya, ob, z2, z2, x, p_a, p_b, w_out)


def _ffn_kernel(h_ref, g1_ref, w1_ref, w2_ref, g2_ref, o_ref, m_ref):
    j = pl.program_id(1)

    @pl.when(j == 0)
    def _():
        x = h_ref[...]
        ms = jnp.mean(x * x, axis=-1, keepdims=True)
        m_ref[...] = (x * lax.rsqrt(ms + RMS_EPS) * g1_ref[...]).astype(BF16)
        o_ref[...] = x

    f = jnp.dot(m_ref[...], w1_ref[...], preferred_element_type=F32)
    f = jnp.square(jnp.maximum(f, 0.0)).astype(BF16)
    o_ref[...] += jnp.dot(f, w2_ref[...], preferred_element_type=F32)

    @pl.when(j == pl.num_programs(1) - 1)
    def _():
        y = o_ref[...]
        ms = jnp.mean(y * y, axis=-1, keepdims=True)
        o_ref[...] = y * lax.rsqrt(ms + RMS_EPS) * g2_ref[...]


def _ffn(h, g1, w1, w2, g2, *, tm, tf):
    m, d = h.shape
    dff = w1.shape[1]
    return pl.pallas_call(
        _ffn_kernel,
        grid=(m // tm, dff // tf),
        in_specs=[pl.BlockSpec((tm, d), lambda i, j: (i, 0)),
                  pl.BlockSpec((1, d), lambda i, j: (0, 0)),
                  pl.BlockSpec((d, tf), lambda i, j: (0, j)),
                  pl.BlockSpec((tf, d), lambda i, j: (j, 0)),
                  pl.BlockSpec((1, d), lambda i, j: (0, 0))],
        out_specs=pl.BlockSpec((tm, d), lambda i, j: (i, 0)),
        out_shape=jax.ShapeDtypeStruct((m, d), F32),
        scratch_shapes=[pltpu.VMEM((tm, d), BF16)],
        compiler_params=_params(("parallel", "arbitrary")),
        name="ffn_residual_norm",
    )(h, g1, w1, w2, g2)


def _pad_cols(w, width):
    return jnp.pad(w, ((0, 0), (0, width - w.shape[1])))


def _pad_rows(w, height):
    return jnp.pad(w, ((0, height - w.shape[0]), (0, 0)))


def kernel(x, norm_mix_g, w_in, mu_shift, w0, w_up, a0, a_up, g_up, k_k, k_a, r_k, lnx_g, lnx_b, lambda_q1, lambda_k1, lambda_q2, lambda_k2, subln_g, rel_bias, p_a, p_b, w_out, norm_mlp_g, w_ff1, w_ff2, norm_final_g):
    b, s, d = x.shape
    depth = w_in.shape[0]
    a_width = w0.shape[1]
    d_lora, a_lora, g_lora = w_up.shape[1], a_up.shape[1], g_up.shape[1]
    n_bheads = rel_bias.shape[1] // 2
    b_width = n_bheads * LANES
    assert depth == 1, "the final norm is fused into the (single) layer's MLP kernel"
    assert d_lora <= LANES and a_lora <= LANES and g_lora == 2 * LANES
    assert a_width % LANES == 0 and s % 512 == 0 and subln_g.shape[1] == LANES
    m = b * s

    o_wd = 3 * a_width
    o_ad = o_wd + d_lora
    o_gd = o_ad + a_lora
    o_b = o_gd + g_lora
    o_g = o_b + 3 * b_width
    rkv_off = 2 * d
    lora_off = rkv_off + 3 * a_width
    bq_off = lora_off + 4 * LANES

    def regroup_cols(w):
        return jnp.concatenate([
            w[:, o_g:], w[:, :o_wd], _pad_cols(w[:, o_wd:o_ad], LANES), _pad_cols(w[:, o_ad:o_gd], LANES),
            w[:, o_gd:o_b], w[:, o_b:o_g]], axis=1)

    h = x.reshape(m, d)
    l = 0
    lambda_init = 0.8 - 0.6 * math.exp(-0.3 * l)
    w_in_r = regroup_cols(w_in[l]).astype(BF16)
    mu = mu_shift[l][None, :]
    mu_l = regroup_cols(jnp.pad(mu, ((0, 0), (0, w_in.shape[2] - mu.shape[1]))))[:, lora_off:bq_off]

    z2 = _norm_matmul(h, norm_mix_g[l][None, :], w_in_r, tm=min(1024, m), tn=512, relu2=False,
                      out_dtype=BF16)
    z3 = z2.reshape(b, s, z2.shape[1])

    ya = _rwkv_branch(
        z3, mu[:, :a_width], mu[:, a_width:2 * a_width], mu[:, 2 * a_width:3 * a_width], mu_l,
        w0[l][None, :], _pad_rows(w_up[l], LANES).astype(BF16), a0[l][None, :],
        _pad_rows(a_up[l], LANES).astype(BF16), g_up[l].astype(BF16),
        k_k[l][None, :], k_a[l][None, :], r_k[l].reshape(1, a_width), lnx_g[l][None, :], lnx_b[l][None, :],
        a_width=a_width, rkv_block=rkv_off // LANES, lora_block=lora_off // (4 * LANES), ts=512,
        out_dtype=BF16)

    ob = _attn_branch(
        z3, _bias_tiles(rel_bias, n_bheads), lambda_q1[l][None, :], lambda_k1[l][None, :],
        lambda_q2[l][None, :], lambda_k2[l][None, :], subln_g[l][None, :],
        q_block=bq_off // LANES, n_heads=n_bheads, lambda_init=lambda_init, out_dtype=BF16)

    h = _merge_out(ya.reshape(m, a_width), ob.reshape(m, b_width), z2, h, p_a[l].astype(BF16),
                   p_b[l].astype(BF16), w_out[l].astype(BF16), gate_block=0, tm=256)
    h = _ffn(h, norm_mlp_g[l][None, :], w_ff1[l].astype(BF16), w_ff2[l].astype(BF16),
             norm_final_g[None, :], tm=512, tf=512)
    return h.reshape(b, s, d)
```

```python
import functools
import math

import jax
import jax.numpy as jnp
from jax import lax
from jax.experimental import pallas as pl
from jax.experimental.pallas import tpu as pltpu

F32 = jnp.float32
BF16 = jnp.bfloat16

LANES = 128
HEAD = 64
CHUNK = 64
Q_BLOCK = 128
CUMSUM_ROWS = 256
RMS_EPS = 1e-6
GN_EPS = 64e-5
SUBLN_EPS = 1e-5
N_BUCKETS = 32
MAX_DISTANCE = 128
NEG_INF = -1e30
LOG2E = 1.4426950408889634
VMEM_LIMIT = 56 * 1024 * 1024


def _mm(a, b):
    return jnp.dot(a.astype(BF16), b.astype(BF16), preferred_element_type=F32)


def _mm_nt(a, b):
    return lax.dot_general(a.astype(BF16), b.astype(BF16), (((1,), (1,)), ((), ())),
                           preferred_element_type=F32)


def _params(sem):
    return pltpu.CompilerParams(dimension_semantics=sem, vmem_limit_bytes=VMEM_LIMIT)


def _norm_matmul_kernel(x_ref, g_ref, w_ref, o_ref, u_ref, *, relu2):
    @pl.when(pl.program_id(1) == 0)
    def _():
        x = x_ref[...]
        ms = jnp.mean(x * x, axis=-1, keepdims=True)
        u_ref[...] = (x * lax.rsqrt(ms + RMS_EPS) * g_ref[...]).astype(BF16)

    acc = jnp.dot(u_ref[...], w_ref[...], preferred_element_type=F32)
    if relu2:
        acc = jnp.square(jnp.maximum(acc, 0.0))
    o_ref[...] = acc.astype(o_ref.dtype)


def _norm_matmul(x, g, w, *, tm, tn, relu2, out_dtype):
    m, d = x.shape
    n = w.shape[1]
    return pl.pallas_call(
        functools.partial(_norm_matmul_kernel, relu2=relu2),
        grid=(m // tm, n // tn),
        in_specs=[pl.BlockSpec((tm, d), lambda i, j: (i, 0)),
                  pl.BlockSpec((1, d), lambda i, j: (0, 0)),
                  pl.BlockSpec((d, tn), lambda i, j: (0, j))],
        out_specs=pl.BlockSpec((tm, tn), lambda i, j: (i, j)),
        out_shape=jax.ShapeDtypeStruct((m, n), out_dtype),
        scratch_shapes=[pltpu.VMEM((tm, d), BF16)],
        compiler_params=_params(("parallel", "arbitrary")),
        name="norm_matmul_relu2" if relu2 else "norm_matmul",
    )(x, g, w)


def _expand(t, lane_lo):
    zero = jnp.zeros_like(t)
    return jnp.concatenate([jnp.where(lane_lo, t, zero), jnp.where(lane_lo, zero, t)], axis=0)


def _sum_heads(x, head_ones):
    xb = x.astype(BF16)
    return jnp.concatenate(
        [jnp.dot(xb[:, p * LANES:(p + 1) * LANES], head_ones, preferred_element_type=F32)
         for p in range(x.shape[1] // LANES)], axis=1)


def _chunk_cumsum(x, tri):
    hi = x.astype(BF16)
    lo = (x - hi.astype(F32)).astype(BF16)
    out = jnp.dot(tri, jnp.concatenate([hi, lo], axis=1), preferred_element_type=F32)
    return out[:, :x.shape[1]] + out[:, x.shape[1]:]


def _rwkv_kernel(zr_ref, zk_ref, zv_ref, zl_ref, mur_ref, muk_ref, muv_ref, mul_ref,
                 w0_ref, wup_ref, a0_ref, aup_ref, gup_ref, kk_ref, ka_ref, rk_ref, lng_ref, lnb_ref,
                 o_ref, cr_ref, ck_ref, cv_ref, cl_ref, h_ref, *, n_chunks, n_pairs):
    c = CHUNK
    c2 = 2 * c
    ts = n_chunks * c
    units = [(p, ch) for ch in range(n_chunks) for p in range(n_pairs)]
    chunks = range(len(units))

    @pl.when(pl.program_id(1) == 0)
    def _():
        cr_ref[...] = jnp.zeros_like(cr_ref)
        ck_ref[...] = jnp.zeros_like(ck_ref)
        cv_ref[...] = jnp.zeros_like(cv_ref)
        cl_ref[...] = jnp.zeros_like(cl_ref)
        h_ref[...] = jnp.zeros_like(h_ref)

    def shifted(z, carry_ref, mu):
        rows = lax.broadcasted_iota(jnp.int32, z.shape, 0)
        prev = jnp.where(rows == 0, carry_ref[...], pltpu.roll(z, 1, 0))
        carry_ref[...] = z[z.shape[0] - 1:, :]
        return z + (prev - z) * mu

    ri = lax.broadcasted_iota(jnp.int32, (c2, c2), 0)
    ci = lax.broadcasted_iota(jnp.int32, (c2, c2), 1)
    strict = ci < ri
    incl = ci <= ri
    eye = (ci == ri).astype(F32)
    head_ones = ((ri // HEAD) == (ci // HEAD)).astype(BF16)
    ti = lax.broadcasted_iota(jnp.int32, (ts, ts), 0)
    tj = lax.broadcasted_iota(jnp.int32, (ts, ts), 1)
    tri = ((tj <= ti) & ((tj // c) == (ti // c))).astype(BF16)
    lane_lo = lax.broadcasted_iota(jnp.int32, (c, LANES), 1) < HEAD

    r = shifted(zr_ref[0].astype(F32), cr_ref, mur_ref[...])
    k = shifted(zk_ref[0].astype(F32), ck_ref, muk_ref[...])
    v = shifted(zv_ref[0].astype(F32), cv_ref, muv_ref[...])
    lo = shifted(zl_ref[0].astype(F32), cl_ref, mul_ref[...])

    t_w = w0_ref[...] + _mm(jnp.tanh(lo[:, 0:LANES]), wup_ref[...])
    lw = -math.exp(-0.5) * jax.nn.sigmoid(t_w)
    a_sig = jax.nn.sigmoid(a0_ref[...] + _mm(lo[:, LANES:2 * LANES], aup_ref[...]))
    gate = _mm(jax.nn.sigmoid(lo[:, 2 * LANES:]), gup_ref[...])
    kk = k * kk_ref[...]
    kk = kk * lax.rsqrt(jnp.maximum(_sum_heads(kk * kk, head_ones), 1e-24))
    k2 = k * (1.0 + (a_sig - 1.0) * ka_ref[...])
    b_v = kk * a_sig
    bonus = _sum_heads(r * k2 * rk_ref[...], head_ones) * v
    cum = _chunk_cumsum(lw, tri)
    g_in = jnp.exp(cum)
    g_inv = jnp.exp(-cum)
    rt_all = r * g_in
    at_all = -kk * jnp.exp(cum - lw)
    kt_all = k2 * g_inv
    bt_all = b_v * g_inv

    def tile(x, u):
        p, ch = units[u]
        return x[ch * c:(ch + 1) * c, p * LANES:(p + 1) * LANES]

    cum_end = [tile(cum, u)[c - 1:, :] for u in chunks]
    g_tail = [jnp.exp(cum_end[u] - tile(cum, u)) for u in chunks]
    rt = [_expand(tile(rt_all, u), lane_lo) for u in chunks]
    at = [_expand(tile(at_all, u), lane_lo) for u in chunks]
    kt = [_expand(tile(kt_all, u), lane_lo) for u in chunks]
    bt = [_expand(tile(bt_all, u), lane_lo) for u in chunks]
    kb = [_expand(tile(k2, u) * g_tail[u], lane_lo) for u in chunks]
    bb = [_expand(tile(b_v, u) * g_tail[u], lane_lo) for u in chunks]
    ve = [_expand(tile(v, u), lane_lo) for u in chunks]

    sc = [_mm_nt(jnp.concatenate([at[ch], rt[ch]], axis=0), jnp.concatenate([bt[ch], kt[ch]], axis=0))
          for ch in chunks]
    n_ab = [jnp.where(strict, sc[ch][:c2, :c2], 0.0) for ch in chunks]
    a_ak = [jnp.where(strict, sc[ch][:c2, c2:], 0.0) for ch in chunks]
    a_rb = [jnp.where(incl, sc[ch][c2:, :c2], 0.0) for ch in chunks]
    a_rk = [jnp.where(incl, sc[ch][c2:, c2:], 0.0) for ch in chunks]

    p = [eye + n_ab[ch] for ch in chunks]
    nk = [_mm(n_ab[ch], n_ab[ch]) for ch in chunks]
    akv = [_mm(a_ak[ch], ve[ch]) for ch in chunks]
    arkv = [_mm(a_rk[ch], ve[ch]) for ch in chunks]
    kbv = [_mm(kb[ch].T, ve[ch]) for ch in chunks]
    power = 2
    while power < c:
        if 2 * power < c:
            both = [_mm(nk[ch], jnp.concatenate([nk[ch], p[ch]], axis=1)) for ch in chunks]
            nk = [both[ch][:, :c2] for ch in chunks]
            p = [p[ch] + both[ch][:, c2:] for ch in chunks]
        else:
            p = [p[ch] + _mm(nk[ch], p[ch]) for ch in chunks]
        power *= 2

    ta = [_mm(p[ch], jnp.concatenate([at[ch], akv[ch]], axis=1)) for ch in chunks]
    rb = [_mm(a_rb[ch], ta[ch]) for ch in chunks]
    mg = [_mm(bb[ch].T, ta[ch]) for ch in chunks]
    r_hat = [rt[ch] + rb[ch][:, :LANES] for ch in chunks]
    y_in = [rb[ch][:, LANES:] + arkv[ch] for ch in chunks]
    m_state = [eye * jnp.exp(cum_end[ch]) + mg[ch][:, :LANES] for ch in chunks]
    g_state = [mg[ch][:, LANES:] + kbv[ch] for ch in chunks]

    h = [h_ref[p] for p in range(n_pairs)]
    ys = [[] for _ in range(n_pairs)]
    for u in chunks:
        p = units[u][0]
        ye = _mm(r_hat[u], h[p]) + y_in[u]
        h[p] = _mm(m_state[u], h[p]) + g_state[u]
        ys[p].append(ye[:c] + ye[c:])
    for p in range(n_pairs):
        h_ref[p] = h[p]
    y = jnp.concatenate([jnp.concatenate(ys[p], axis=0) for p in range(n_pairs)], axis=1)

    mean = _sum_heads(y, head_ones) * (1.0 / HEAD)
    d = y - mean
    var = _sum_heads(d * d, head_ones) * (1.0 / HEAD)
    yn = d * lax.rsqrt(var + GN_EPS) * lng_ref[...] + lnb_ref[...]
    o_ref[0] = ((yn + bonus) * gate).astype(o_ref.dtype)


def _rwkv_branch(z3, mu_r, mu_k, mu_v, mu_l, w0, wup, a0, aup, gup, k_k, k_a, r_k, lnx_g, lnx_b,
                 *, a_width, rkv_block, lora_block, ts, out_dtype):
    b, s, _ = z3.shape
    n_pairs = a_width // LANES
    lw = 4 * LANES

    def col(blk):
        return lambda bi, si: (bi, si, blk)

    def full(rows, width):
        return pl.BlockSpec((rows, width), lambda bi, si: (0, 0))

    in_specs = [
        pl.BlockSpec((1, ts, a_width), col(rkv_block)),
        pl.BlockSpec((1, ts, a_width), col(rkv_block + 1)),
        pl.BlockSpec((1, ts, a_width), col(rkv_block + 2)),
        pl.BlockSpec((1, ts, lw), col(lora_block)),
        full(1, a_width), full(1, a_width), full(1, a_width), full(1, lw),
        full(1, a_width), full(LANES, a_width), full(1, a_width), full(LANES, a_width),
        full(2 * LANES, a_width),
        full(1, a_width), full(1, a_width), full(1, a_width), full(1, a_width), full(1, a_width),
    ]
    return pl.pallas_call(
        functools.partial(_rwkv_kernel, n_chunks=ts // CHUNK, n_pairs=n_pairs),
        grid=(b, s // ts),
        in_specs=in_specs,
        out_specs=pl.BlockSpec((1, ts, a_width), lambda bi, si: (bi, si, 0)),
        out_shape=jax.ShapeDtypeStruct((b, s, a_width), out_dtype),
        scratch_shapes=[pltpu.VMEM((1, a_width), F32), pltpu.VMEM((1, a_width), F32),
                        pltpu.VMEM((1, a_width), F32), pltpu.VMEM((1, lw), F32),
                        pltpu.VMEM((n_pairs, LANES, LANES), F32)],
        compiler_params=_params(("parallel", "arbitrary")),
        name="rwkv7_chunked",
    )(z3, z3, z3, z3, mu_r, mu_k, mu_v, mu_l, w0, wup, a0, aup, gup, k_k, k_a, r_k, lnx_g, lnx_b)


def _attn_kernel(q_ref, k_ref, v_ref, bias_ref, lq1_ref, lk1_ref, lq2_ref, lk2_ref, sg_ref, o_ref,
                 s_ref, p_ref, va_ref, *, n_qblocks, lambda_init):
    qr = Q_BLOCK
    lam = (jnp.exp(jnp.sum(lq1_ref[...] * lk1_ref[...])) - jnp.exp(jnp.sum(lq2_ref[...] * lk2_ref[...]))
           + lambda_init)
    lane_lo = lax.broadcasted_iota(jnp.int32, (qr, LANES), 1) < HEAD
    scale = HEAD ** -0.5 * LOG2E
    va_ref[:, :LANES] = v_ref[0].astype(BF16)
    va_ref[:, LANES:] = jnp.ones((va_ref.shape[0], LANES), BF16)

    def key_tiles(qb):
        return [(j, min(2, qb + 1 - j)) for j in range(0, qb + 1, 2)]

    def logits(qb):
        q = (q_ref[0, qb * qr:(qb + 1) * qr, :].astype(F32) * scale).astype(BF16)
        qe = _expand(q, lane_lo)
        m_l = None
        for j0, nb in key_tiles(qb):
            s = _mm_nt(qe, k_ref[0, j0 * qr:(j0 + nb) * qr, :])
            for t in range(nb):
                j = j0 + t
                sj = s[:, t * qr:(t + 1) * qr]
                if qb - j < 2:
                    sj = sj + bias_ref[0, qb - j]
                s_ref[qb % 2, :, j * qr:(j + 1) * qr] = sj
                m_l = sj if m_l is None else jnp.maximum(m_l, sj)
        return m_l

    def weighted_values(qb, m_l):
        m = jnp.max(m_l, axis=-1, keepdims=True)
        for j in range(qb + 1):
            pj = jnp.exp2(s_ref[qb % 2, :, j * qr:(j + 1) * qr] - m)
            p_ref[qb % 2, :, j * qr:(j + 1) * qr] = pj.astype(BF16)
        n_keys = (qb + 1) * qr
        acc = jnp.dot(p_ref[qb % 2, :, :n_keys], va_ref[:n_keys, :], preferred_element_type=F32)
        o = acc[:, :LANES] / acc[:, LANES:]
        o = o[:qr] - lam * o[qr:]
        o = o * lax.rsqrt(jnp.mean(o * o, axis=-1, keepdims=True) + SUBLN_EPS) * sg_ref[...]
        o = o * (1.0 - lambda_init)
        o_ref[0, qb * qr:(qb + 1) * qr, :] = o.astype(o_ref.dtype)

    pending = logits(0)
    for qb in range(n_qblocks):
        nxt = logits(qb + 1) if qb + 1 < n_qblocks else None
        weighted_values(qb, pending)
        pending = nxt


def _attn_branch(z3, bias, lq1, lk1, lq2, lk2, subln_g, *, q_block, n_heads, lambda_init, out_dtype):
    b, s, _ = z3.shape

    def col(off):
        return lambda bi, hh: (bi, 0, off + hh)

    def small(n):
        return pl.BlockSpec((1, n), lambda bi, hh: (0, 0))

    return pl.pallas_call(
        functools.partial(_attn_kernel, n_qblocks=s // Q_BLOCK, lambda_init=lambda_init),
        grid=(b, n_heads),
        in_specs=[pl.BlockSpec((1, s, LANES), col(q_block)),
                  pl.BlockSpec((1, s, LANES), col(q_block + n_heads)),
                  pl.BlockSpec((1, s, LANES), col(q_block + 2 * n_heads)),
                  pl.BlockSpec((1, 2, 2 * Q_BLOCK, Q_BLOCK), lambda bi, hh: (hh, 0, 0, 0)),
                  small(HEAD), small(HEAD), small(HEAD), small(HEAD), small(LANES)],
        out_specs=pl.BlockSpec((1, s, LANES), lambda bi, hh: (bi, 0, hh)),
        out_shape=jax.ShapeDtypeStruct((b, s, n_heads * LANES), out_dtype),
        scratch_shapes=[pltpu.VMEM((2, 2 * Q_BLOCK, s), F32), pltpu.VMEM((2, 2 * Q_BLOCK, s), BF16),
                        pltpu.VMEM((s, 2 * LANES), BF16)],
        compiler_params=_params(("parallel", "parallel")),
        name="diff_attention",
    )(z3, z3, z3, bias, lq1, lk1, lq2, lk2, subln_g)


def _t5_bucket(rel):
    nb = N_BUCKETS // 2
    max_exact = nb // 2
    ret = jnp.where(rel > 0, nb, 0)
    n = jnp.abs(rel)
    nf = jnp.maximum(n, 1).astype(jnp.float32)
    large = max_exact + (jnp.log(nf / max_exact) / math.log(MAX_DISTANCE / max_exact)
                         * (nb - max_exact)).astype(jnp.int32)
    large = jnp.minimum(large, nb - 1)
    return ret + jnp.where(n < max_exact, n, large)


def _bias_tiles(rel_bias, n_heads):
    assert Q_BLOCK >= MAX_DISTANCE
    dist = jnp.arange(3)[:, None, None]
    qi = jnp.arange(Q_BLOCK)[None, :, None]
    kj = jnp.arange(Q_BLOCK)[None, None, :]
    bucket = _t5_bucket(kj - qi - dist * Q_BLOCK)
    onehot = (bucket[..., None] == jnp.arange(N_BUCKETS)).astype(F32)
    t = jnp.einsum("dqkn,nh->hdqk", onehot, rel_bias.astype(F32), precision=lax.Precision.HIGHEST)
    allowed = (dist > 0) | ((kj // CHUNK) <= (qi // CHUNK))
    t = jnp.where(allowed[None], t, NEG_INF)
    t = (t[:, :2] - t[:, 2:]) * LOG2E
    t = t.reshape(n_heads, 2, 2, Q_BLOCK, Q_BLOCK).transpose(0, 2, 1, 3, 4)
    return t.reshape(n_heads, 2, 2 * Q_BLOCK, Q_BLOCK)


def _merge_out_kernel(ya_ref, ob_ref, ga_ref, gb_ref, x_ref, pa_ref, pb_ref, wo_ref, o_ref):
    oa = jnp.dot(ya_ref[...], pa_ref[...], preferred_element_type=F32)
    ob = jnp.dot(ob_ref[...], pb_ref[...], preferred_element_type=F32)
    merged = (jax.nn.sigmoid(ga_ref[...].astype(F32)) * oa
              + jax.nn.sigmoid(gb_ref[...].astype(F32)) * ob)
    o_ref[...] = x_ref[...] + jnp.dot(merged.astype(BF16), wo_ref[...], preferred_element_type=F32)


def _merge_out(ya, ob, z2, x, p_a, p_b, w_out, *, gate_block, tm):
    m, ka = ya.shape
    kb = ob.shape[1]
    d = w_out.shape[1]

    def resident(shape):
        return pl.BlockSpec(shape, lambda i: (0, 0), pipeline_mode=pl.Buffered(1))

    return pl.pallas_call(
        _merge_out_kernel,
        grid=(m // tm,),
        in_specs=[pl.BlockSpec((tm, ka), lambda i: (i, 0)),
                  pl.BlockSpec((tm, kb), lambda i: (i, 0)),
                  pl.BlockSpec((tm, d), lambda i: (i, gate_block)),
                  pl.BlockSpec((tm, d), lambda i: (i, gate_block + 1)),
                  pl.BlockSpec((tm, d), lambda i: (i, 0)),
                  resident(p_a.shape), resident(p_b.shape), resident(w_out.shape)],
        out_specs=pl.BlockSpec((tm, d), lambda i: (i, 0)),
        out_shape=jax.ShapeDtypeStruct((m, d), F32),
        compiler_params=_params(("parallel",)),
        name="merge_out_proj",
    )(ya, ob, z2, z2, x, p_a, p_b, w_out)


def _ffn_kernel(h_ref, g1_ref, w1_ref, w2_ref, g2_ref, o_ref, m_ref):
    j = pl.program_id(1)

    @pl.when(j == 0)
    def _():
        x = h_ref[...]
        ms = jnp.mean(x * x, axis=-1, keepdims=True)
        m_ref[...] = (x * lax.rsqrt(ms + RMS_EPS) * g1_ref[...]).astype(BF16)
        o_ref[...] = x

    f = jnp.dot(m_ref[...], w1_ref[...], preferred_element_type=F32)
    f = jnp.square(jnp.maximum(f, 0.0)).astype(BF16)
    o_ref[...] += jnp.dot(f, w2_ref[...], preferred_element_type=F32)

    @pl.when(j == pl.num_programs(1) - 1)
    def _():
        y = o_ref[...]
        ms = jnp.mean(y * y, axis=-1, keepdims=True)
        o_ref[...] = y * lax.rsqrt(ms + RMS_EPS) * g2_ref[...]


def _ffn(h, g1, w1, w2, g2, *, tm, tf):
    m, d = h.shape
    dff = w1.shape[1]
    return pl.pallas_call(
        _ffn_kernel,
        grid=(m // tm, dff // tf),
        in_specs=[pl.BlockSpec((tm, d), lambda i, j: (i, 0)),
                  pl.BlockSpec((1, d), lambda i, j: (0, 0)),
                  pl.BlockSpec((d, tf), lambda i, j: (0, j)),
                  pl.BlockSpec((tf, d), lambda i, j: (j, 0)),
                  pl.BlockSpec((1, d), lambda i, j: (0, 0))],
        out_specs=pl.BlockSpec((tm, d), lambda i, j: (i, 0)),
        out_shape=jax.ShapeDtypeStruct((m, d), F32),
        scratch_shapes=[pltpu.VMEM((tm, d), BF16)],
        compiler_params=_params(("parallel", "arbitrary")),
        name="ffn_residual_norm",
    )(h, g1, w1, w2, g2)


def _pad_cols(w, width):
    return jnp.pad(w, ((0, 0), (0, width - w.shape[1])))


def _pad_rows(w, height):
    return jnp.pad(w, ((0, height - w.shape[0]), (0, 0)))


def kernel(x, norm_mix_g, w_in, mu_shift, w0, w_up, a0, a_up, g_up, k_k, k_a, r_k, lnx_g, lnx_b, lambda_q1, lambda_k1, lambda_q2, lambda_k2, subln_g, rel_bias, p_a, p_b, w_out, norm_mlp_g, w_ff1, w_ff2, norm_final_g):
    b, s, d = x.shape
    depth = w_in.shape[0]
    a_width = w0.shape[1]
    d_lora, a_lora, g_lora = w_up.shape[1], a_up.shape[1], g_up.shape[1]
    n_bheads = rel_bias.shape[1] // 2
    b_width = n_bheads * LANES
    assert depth == 1, "the final norm is fused into the (single) layer's MLP kernel"
    assert d_lora <= LANES and a_lora <= LANES and g_lora == 2 * LANES
    assert a_width % LANES == 0 and s % 512 == 0 and subln_g.shape[1] == LANES
    m = b * s

    o_wd = 3 * a_width
    o_ad = o_wd + d_lora
    o_gd = o_ad + a_lora
    o_b = o_gd + g_lora
    o_g = o_b + 3 * b_width
    rkv_off = 2 * d
    lora_off = rkv_off + 3 * a_width
    bq_off = lora_off + 4 * LANES

    def regroup_cols(w):
        return jnp.concatenate([
            w[:, o_g:], w[:, :o_wd], _pad_cols(w[:, o_wd:o_ad], LANES), _pad_cols(w[:, o_ad:o_gd], LANES),
            w[:, o_gd:o_b], w[:, o_b:o_g]], axis=1)

    h = x.reshape(m, d)
    l = 0
    lambda_init = 0.8 - 0.6 * math.exp(-0.3 * l)
    w_in_r = regroup_cols(w_in[l]).astype(BF16)
    mu = mu_shift[l][None, :]
    mu_l = regroup_cols(jnp.pad(mu, ((0, 0), (0, w_in.shape[2] - mu.shape[1]))))[:, lora_off:bq_off]

    z2 = _norm_matmul(h, norm_mix_g[l][None, :], w_in_r, tm=min(1024, m), tn=1536, relu2=False,
                      out_dtype=BF16)
    z3 = z2.reshape(b, s, z2.shape[1])

    ya = _rwkv_branch(
        z3, mu[:, :a_width], mu[:, a_width:2 * a_width], mu[:, 2 * a_width:3 * a_width], mu_l,
        w0[l][None, :], _pad_rows(w_up[l], LANES).astype(BF16), a0[l][None, :],
        _pad_rows(a_up[l], LANES).astype(BF16), g_up[l].astype(BF16),
        k_k[l][None, :], k_a[l][None, :], r_k[l].reshape(1, a_width), lnx_g[l][None, :], lnx_b[l][None, :],
        a_width=a_width, rkv_block=rkv_off // a_width, lora_block=lora_off // (4 * LANES), ts=2 * CHUNK,
        out_dtype=BF16)

    ob = _attn_branch(
        z3, _bias_tiles(rel_bias, n_bheads), lambda_q1[l][None, :], lambda_k1[l][None, :],
        lambda_q2[l][None, :], lambda_k2[l][None, :], subln_g[l][None, :],
        q_block=bq_off // LANES, n_heads=n_bheads, lambda_init=lambda_init, out_dtype=BF16)

    h = _merge_out(ya.reshape(m, a_width), ob.reshape(m, b_width), z2, h, p_a[l].astype(BF16),
                   p_b[l].astype(BF16), w_out[l].astype(BF16), gate_block=0, tm=256)
    h = _ffn(h, norm_mlp_g[l][None, :], w_ff1[l].astype(BF16), w_ff2[l].astype(BF16),
             norm_final_g[None, :], tm=512, tf=1024)
    return h.reshape(b, s, d)
```

```python
import functools
import math

import jax
import jax.numpy as jnp
from jax import lax
from jax.experimental import pallas as pl
from jax.experimental.pallas import tpu as pltpu

F32 = jnp.float32
BF16 = jnp.bfloat16

LANES = 128
HEAD = 64
CHUNK = 64
Q_BLOCK = 128
MXU_DIM = 256
RMS_EPS = 1e-6
GN_EPS = 64e-5
SUBLN_EPS = 1e-5
N_BUCKETS = 32
MAX_DISTANCE = 128
NEG_INF = -1e30
LOG2E = 1.4426950408889634
VMEM_LIMIT = 56 * 1024 * 1024


def _mm(a, b):
    return jnp.dot(a.astype(BF16), b.astype(BF16), preferred_element_type=F32)


def _mm_nt(a, b):
    return lax.dot_general(a.astype(BF16), b.astype(BF16), (((1,), (1,)), ((), ())),
                           preferred_element_type=F32)


def _params(sem):
    return pltpu.CompilerParams(dimension_semantics=sem, vmem_limit_bytes=VMEM_LIMIT)


def _norm_matmul_kernel(x_ref, g_ref, w_ref, o_ref, u_ref, *, relu2):
    @pl.when(pl.program_id(1) == 0)
    def _():
        x = x_ref[...]
        ms = jnp.mean(x * x, axis=-1, keepdims=True)
        u_ref[...] = (x * lax.rsqrt(ms + RMS_EPS) * g_ref[...]).astype(BF16)

    acc = jnp.dot(u_ref[...], w_ref[...], preferred_element_type=F32)
    if relu2:
        acc = jnp.square(jnp.maximum(acc, 0.0))
    o_ref[...] = acc.astype(o_ref.dtype)


def _norm_matmul(x, g, w, *, tm, tn, relu2, out_dtype):
    m, d = x.shape
    n = w.shape[1]
    return pl.pallas_call(
        functools.partial(_norm_matmul_kernel, relu2=relu2),
        grid=(m // tm, n // tn),
        in_specs=[pl.BlockSpec((tm, d), lambda i, j: (i, 0)),
                  pl.BlockSpec((1, d), lambda i, j: (0, 0)),
                  pl.BlockSpec((d, tn), lambda i, j: (0, j))],
        out_specs=pl.BlockSpec((tm, tn), lambda i, j: (i, j)),
        out_shape=jax.ShapeDtypeStruct((m, n), out_dtype),
        scratch_shapes=[pltpu.VMEM((tm, d), BF16)],
        compiler_params=_params(("parallel", "arbitrary")),
        name="norm_matmul_relu2" if relu2 else "norm_matmul",
    )(x, g, w)


def _expand(t, lane_lo):
    zero = jnp.zeros_like(t)
    return jnp.concatenate([jnp.where(lane_lo, t, zero), jnp.where(lane_lo, zero, t)], axis=0)


def _sum_heads(x, head_ones):
    xb = x.astype(BF16)
    g = head_ones.shape[0]
    return jnp.concatenate(
        [jnp.dot(xb[:, i * g:(i + 1) * g], head_ones, preferred_element_type=F32)
         for i in range(x.shape[1] // g)], axis=1)


def _chunk_cumsum(x, tri):
    hi = x.astype(BF16)
    lo = (x - hi.astype(F32)).astype(BF16)
    out = jnp.dot(tri, jnp.concatenate([hi, lo], axis=1), preferred_element_type=F32)
    return out[:, :x.shape[1]] + out[:, x.shape[1]:]


def _rwkv_kernel(zr_ref, zk_ref, zv_ref, zl_ref, mur_ref, muk_ref, muv_ref, mul_ref,
                 w0_ref, wup_ref, a0_ref, aup_ref, gup_ref, kk_ref, ka_ref, rk_ref, lng_ref, lnb_ref,
                 o_ref, cr_ref, ck_ref, cv_ref, cl_ref, h_ref, *, n_chunks, n_pairs):
    c = CHUNK
    c2 = 2 * c
    ts = n_chunks * c
    units = [(p, ch) for ch in range(n_chunks) for p in range(n_pairs)]
    chunks = range(len(units))

    @pl.when(pl.program_id(1) == 0)
    def _():
        cr_ref[...] = jnp.zeros_like(cr_ref)
        ck_ref[...] = jnp.zeros_like(ck_ref)
        cv_ref[...] = jnp.zeros_like(cv_ref)
        cl_ref[...] = jnp.zeros_like(cl_ref)
        h_ref[...] = jnp.zeros_like(h_ref)

    def shifted(z, carry_ref, mu):
        rows = lax.broadcasted_iota(jnp.int32, z.shape, 0)
        prev = jnp.where(rows == 0, carry_ref[...], pltpu.roll(z, 1, 0))
        carry_ref[...] = z[z.shape[0] - 1:, :]
        return z + (prev - z) * mu

    ri = lax.broadcasted_iota(jnp.int32, (c2, c2), 0)
    ci = lax.broadcasted_iota(jnp.int32, (c2, c2), 1)
    strict = ci < ri
    incl = ci <= ri
    eye = (ci == ri).astype(F32)
    hr = lax.broadcasted_iota(jnp.int32, (MXU_DIM, MXU_DIM), 0)
    hc = lax.broadcasted_iota(jnp.int32, (MXU_DIM, MXU_DIM), 1)
    head_ones = ((hr // HEAD) == (hc // HEAD)).astype(BF16)
    ti = lax.broadcasted_iota(jnp.int32, (ts, ts), 0)
    tj = lax.broadcasted_iota(jnp.int32, (ts, ts), 1)
    tri = ((tj <= ti) & ((tj // c) == (ti // c))).astype(BF16)
    lane_lo = lax.broadcasted_iota(jnp.int32, (c, LANES), 1) < HEAD

    r = shifted(zr_ref[0].astype(F32), cr_ref, mur_ref[...])
    k = shifted(zk_ref[0].astype(F32), ck_ref, muk_ref[...])
    v = shifted(zv_ref[0].astype(F32), cv_ref, muv_ref[...])
    lo = shifted(zl_ref[0].astype(F32), cl_ref, mul_ref[...])

    t_w = w0_ref[...] + _mm(jnp.tanh(lo[:, 0:LANES]), wup_ref[...])
    lw = -math.exp(-0.5) * jax.nn.sigmoid(t_w)
    a_sig = jax.nn.sigmoid(a0_ref[...] + _mm(lo[:, LANES:2 * LANES], aup_ref[...]))
    gate = _mm(jax.nn.sigmoid(lo[:, 2 * LANES:]), gup_ref[...])
    kk = k * kk_ref[...]
    kk = kk * lax.rsqrt(jnp.maximum(_sum_heads(kk * kk, head_ones), 1e-24))
    k2 = k * (1.0 + (a_sig - 1.0) * ka_ref[...])
    b_v = kk * a_sig
    bonus = _sum_heads(r * k2 * rk_ref[...], head_ones) * v
    cum = _chunk_cumsum(lw, tri)
    g_in = jnp.exp(cum)
    g_inv = jnp.exp(-cum)
    rt_all = r * g_in
    at_all = -kk * jnp.exp(cum - lw)
    kt_all = k2 * g_inv
    bt_all = b_v * g_inv

    def tile(x, u):
        p, ch = units[u]
        return x[ch * c:(ch + 1) * c, p * LANES:(p + 1) * LANES]

    cum_end = [tile(cum, u)[c - 1:, :] for u in chunks]
    g_tail = [jnp.exp(cum_end[u] - tile(cum, u)) for u in chunks]
    rt = [_expand(tile(rt_all, u), lane_lo) for u in chunks]
    at = [_expand(tile(at_all, u), lane_lo) for u in chunks]
    kt = [_expand(tile(kt_all, u), lane_lo) for u in chunks]
    bt = [_expand(tile(bt_all, u), lane_lo) for u in chunks]
    kb = [_expand(tile(k2, u) * g_tail[u], lane_lo) for u in chunks]
    bb = [_expand(tile(b_v, u) * g_tail[u], lane_lo) for u in chunks]
    ve = [_expand(tile(v, u), lane_lo) for u in chunks]

    sc = [_mm_nt(jnp.concatenate([at[ch], rt[ch]], axis=0), jnp.concatenate([bt[ch], kt[ch]], axis=0))
          for ch in chunks]
    n_ab = [jnp.where(strict, sc[ch][:c2, :c2], 0.0) for ch in chunks]
    a_ak = [jnp.where(strict, sc[ch][:c2, c2:], 0.0) for ch in chunks]
    a_rb = [jnp.where(incl, sc[ch][c2:, :c2], 0.0) for ch in chunks]
    a_rk = [jnp.where(incl, sc[ch][c2:, c2:], 0.0) for ch in chunks]

    p = [eye + n_ab[ch] for ch in chunks]
    nk = [_mm(n_ab[ch], n_ab[ch]) for ch in chunks]
    akv = [_mm(a_ak[ch], ve[ch]) for ch in chunks]
    power = 2
    while power < c:
        if 2 * power < c:
            both = [_mm(nk[ch], jnp.concatenate([nk[ch], p[ch]], axis=1)) for ch in chunks]
            nk = [both[ch][:, :c2] for ch in chunks]
            p = [p[ch] + both[ch][:, c2:] for ch in chunks]
        else:
            p = [p[ch] + _mm(nk[ch], p[ch]) for ch in chunks]
        power *= 2

    ta = [_mm(p[ch], jnp.concatenate([at[ch], akv[ch]], axis=1)) for ch in chunks]
    zero = jnp.zeros((c2, LANES), F32)
    rhs = [jnp.concatenate([ta[ch], jnp.concatenate([zero, ve[ch]], axis=1)], axis=0) for ch in chunks]
    lhs = [jnp.concatenate([jnp.concatenate([a_rb[ch], a_rk[ch]], axis=1),
                            jnp.concatenate([bb[ch].T, kb[ch].T], axis=1)], axis=0) for ch in chunks]
    out = [_mm(lhs[ch], rhs[ch]) for ch in chunks]
    r_hat = [rt[ch] + out[ch][:c2, :LANES] for ch in chunks]
    y_in = [out[ch][:c2, LANES:] for ch in chunks]
    m_state = [eye * jnp.exp(cum_end[ch]) + out[ch][c2:, :LANES] for ch in chunks]
    g_state = [out[ch][c2:, LANES:] for ch in chunks]

    h = [h_ref[p] for p in range(n_pairs)]
    ys = [[] for _ in range(n_pairs)]
    for u in chunks:
        p = units[u][0]
        ye = _mm(r_hat[u], h[p]) + y_in[u]
        h[p] = _mm(m_state[u], h[p]) + g_state[u]
        ys[p].append(ye[:c] + ye[c:])
    for p in range(n_pairs):
        h_ref[p] = h[p]
    y = jnp.concatenate([jnp.concatenate(ys[p], axis=0) for p in range(n_pairs)], axis=1)

    mean = _sum_heads(y, head_ones) * (1.0 / HEAD)
    d = y - mean
    var = _sum_heads(d * d, head_ones) * (1.0 / HEAD)
    yn = d * lax.rsqrt(var + GN_EPS) * lng_ref[...] + lnb_ref[...]
    o_ref[0] = ((yn + bonus) * gate).astype(o_ref.dtype)


def _rwkv_branch(z3, mu_r, mu_k, mu_v, mu_l, w0, wup, a0, aup, gup, k_k, k_a, r_k, lnx_g, lnx_b,
                 *, a_width, rkv_block, lora_block, ts, out_dtype):
    b, s, _ = z3.shape
    n_pairs = a_width // LANES
    lw = 4 * LANES

    def col(blk):
        return lambda bi, si: (bi, si, blk)

    def full(rows, width):
        return pl.BlockSpec((rows, width), lambda bi, si: (0, 0))

    in_specs = [
        pl.BlockSpec((1, ts, a_width), col(rkv_block)),
        pl.BlockSpec((1, ts, a_width), col(rkv_block + 1)),
        pl.BlockSpec((1, ts, a_width), col(rkv_block + 2)),
        pl.BlockSpec((1, ts, lw), col(lora_block)),
        full(1, a_width), full(1, a_width), full(1, a_width), full(1, lw),
        full(1, a_width), full(LANES, a_width), full(1, a_width), full(LANES, a_width),
        full(2 * LANES, a_width),
        full(1, a_width), full(1, a_width), full(1, a_width), full(1, a_width), full(1, a_width),
    ]
    return pl.pallas_call(
        functools.partial(_rwkv_kernel, n_chunks=ts // CHUNK, n_pairs=n_pairs),
        grid=(b, s // ts),
        in_specs=in_specs,
        out_specs=pl.BlockSpec((1, ts, a_width), lambda bi, si: (bi, si, 0)),
        out_shape=jax.ShapeDtypeStruct((b, s, a_width), out_dtype),
        scratch_shapes=[pltpu.VMEM((1, a_width), F32), pltpu.VMEM((1, a_width), F32),
                        pltpu.VMEM((1, a_width), F32), pltpu.VMEM((1, lw), F32),
                        pltpu.VMEM((n_pairs, LANES, LANES), F32)],
        compiler_params=_params(("parallel", "arbitrary")),
        name="rwkv7_chunked",
    )(z3, z3, z3, z3, mu_r, mu_k, mu_v, mu_l, w0, wup, a0, aup, gup, k_k, k_a, r_k, lnx_g, lnx_b)


def _attn_kernel(q_ref, k_ref, v_ref, bias_ref, lq1_ref, lk1_ref, lq2_ref, lk2_ref, sg_ref, o_ref,
                 s_ref, p_ref, va_ref, *, n_qblocks, lambda_init):
    qr = Q_BLOCK
    lam = (jnp.exp(jnp.sum(lq1_ref[...] * lk1_ref[...])) - jnp.exp(jnp.sum(lq2_ref[...] * lk2_ref[...]))
           + lambda_init)
    lane_lo = lax.broadcasted_iota(jnp.int32, (qr, LANES), 1) < HEAD
    scale = HEAD ** -0.5 * LOG2E
    va_ref[:, :LANES] = v_ref[0].astype(BF16)
    va_ref[:, LANES:] = jnp.ones((va_ref.shape[0], LANES), BF16)

    def key_tiles(qb):
        return [(j, min(2, qb + 1 - j)) for j in range(0, qb + 1, 2)]

    def logits(qb):
        q = (q_ref[0, qb * qr:(qb + 1) * qr, :].astype(F32) * scale).astype(BF16)
        qe = _expand(q, lane_lo)
        m_l = None
        for j0, nb in key_tiles(qb):
            s = _mm_nt(qe, k_ref[0, j0 * qr:(j0 + nb) * qr, :])
            for t in range(nb):
                j = j0 + t
                sj = s[:, t * qr:(t + 1) * qr]
                if qb - j < 2:
                    sj = sj + bias_ref[0, qb - j]
                s_ref[qb % 2, :, j * qr:(j + 1) * qr] = sj
                m_l = sj if m_l is None else jnp.maximum(m_l, sj)
        return m_l

    def weighted_values(qb, m_l):
        m = jnp.max(m_l, axis=-1, keepdims=True)
        for j in range(qb + 1):
            pj = jnp.exp2(s_ref[qb % 2, :, j * qr:(j + 1) * qr] - m)
            p_ref[qb % 2, :, j * qr:(j + 1) * qr] = pj.astype(BF16)
        n_keys = (qb + 1) * qr
        acc = jnp.dot(p_ref[qb % 2, :, :n_keys], va_ref[:n_keys, :], preferred_element_type=F32)
        o = acc[:, :LANES] / acc[:, LANES:]
        o = o[:qr] - lam * o[qr:]
        o = o * lax.rsqrt(jnp.mean(o * o, axis=-1, keepdims=True) + SUBLN_EPS) * sg_ref[...]
        o = o * (1.0 - lambda_init)
        o_ref[0, qb * qr:(qb + 1) * qr, :] = o.astype(o_ref.dtype)

    pending = logits(0)
    for qb in range(n_qblocks):
        nxt = logits(qb + 1) if qb + 1 < n_qblocks else None
        weighted_values(qb, pending)
        pending = nxt


def _attn_branch(z3, bias, lq1, lk1, lq2, lk2, subln_g, *, q_block, n_heads, lambda_init, out_dtype):
    b, s, _ = z3.shape

    def col(off):
        return lambda bi, hh: (bi, 0, off + hh)

    def small(n):
        return pl.BlockSpec((1, n), lambda bi, hh: (0, 0))

    return pl.pallas_call(
        functools.partial(_attn_kernel, n_qblocks=s // Q_BLOCK, lambda_init=lambda_init),
        grid=(b, n_heads),
        in_specs=[pl.BlockSpec((1, s, LANES), col(q_block)),
                  pl.BlockSpec((1, s, LANES), col(q_block + n_heads)),
                  pl.BlockSpec((1, s, LANES), col(q_block + 2 * n_heads)),
                  pl.BlockSpec((1, 2, 2 * Q_BLOCK, Q_BLOCK), lambda bi, hh: (hh, 0, 0, 0)),
                  small(HEAD), small(HEAD), small(HEAD), small(HEAD), small(LANES)],
        out_specs=pl.BlockSpec((1, s, LANES), lambda bi, hh: (bi, 0, hh)),
        out_shape=jax.ShapeDtypeStruct((b, s, n_heads * LANES), out_dtype),
        scratch_shapes=[pltpu.VMEM((2, 2 * Q_BLOCK, s), F32), pltpu.VMEM((2, 2 * Q_BLOCK, s), BF16),
                        pltpu.VMEM((s, 2 * LANES), BF16)],
        compiler_params=_params(("parallel", "parallel")),
        name="diff_attention",
    )(z3, z3, z3, bias, lq1, lk1, lq2, lk2, subln_g)


def _t5_bucket(rel):
    nb = N_BUCKETS // 2
    max_exact = nb // 2
    ret = jnp.where(rel > 0, nb, 0)
    n = jnp.abs(rel)
    nf = jnp.maximum(n, 1).astype(jnp.float32)
    large = max_exact + (jnp.log(nf / max_exact) / math.log(MAX_DISTANCE / max_exact)
                         * (nb - max_exact)).astype(jnp.int32)
    large = jnp.minimum(large, nb - 1)
    return ret + jnp.where(n < max_exact, n, large)


def _bias_tiles(rel_bias, n_heads):
    assert Q_BLOCK >= MAX_DISTANCE
    dist = jnp.arange(3)[:, None, None]
    qi = jnp.arange(Q_BLOCK)[None, :, None]
    kj = jnp.arange(Q_BLOCK)[None, None, :]
    bucket = _t5_bucket(kj - qi - dist * Q_BLOCK)
    onehot = (bucket[..., None] == jnp.arange(N_BUCKETS)).astype(F32)
    t = jnp.einsum("dqkn,nh->hdqk", onehot, rel_bias.astype(F32), precision=lax.Precision.HIGHEST)
    allowed = (dist > 0) | ((kj // CHUNK) <= (qi // CHUNK))
    t = jnp.where(allowed[None], t, NEG_INF)
    t = (t[:, :2] - t[:, 2:]) * LOG2E
    t = t.reshape(n_heads, 2, 2, Q_BLOCK, Q_BLOCK).transpose(0, 2, 1, 3, 4)
    return t.reshape(n_heads, 2, 2 * Q_BLOCK, Q_BLOCK)


def _merge_out_kernel(ya_ref, ob_ref, ga_ref, gb_ref, x_ref, pa_ref, pb_ref, wo_ref, o_ref):
    oa = jnp.dot(ya_ref[...], pa_ref[...], preferred_element_type=F32)
    ob = jnp.dot(ob_ref[...], pb_ref[...], preferred_element_type=F32)
    merged = (jax.nn.sigmoid(ga_ref[...].astype(F32)) * oa
              + jax.nn.sigmoid(gb_ref[...].astype(F32)) * ob)
    o_ref[...] = x_ref[...] + jnp.dot(merged.astype(BF16), wo_ref[...], preferred_element_type=F32)


def _merge_out(ya, ob, z2, x, p_a, p_b, w_out, *, gate_block, tm):
    m, ka = ya.shape
    kb = ob.shape[1]
    d = w_out.shape[1]

    def resident(shape):
        return pl.BlockSpec(shape, lambda i: (0, 0), pipeline_mode=pl.Buffered(1))

    return pl.pallas_call(
        _merge_out_kernel,
        grid=(m // tm,),
        in_specs=[pl.BlockSpec((tm, ka), lambda i: (i, 0)),
                  pl.BlockSpec((tm, kb), lambda i: (i, 0)),
                  pl.BlockSpec((tm, d), lambda i: (i, gate_block)),
                  pl.BlockSpec((tm, d), lambda i: (i, gate_block + 1)),
                  pl.BlockSpec((tm, d), lambda i: (i, 0)),
                  resident(p_a.shape), resident(p_b.shape), resident(w_out.shape)],
        out_specs=pl.BlockSpec((tm, d), lambda i: (i, 0)),
        out_shape=jax.ShapeDtypeStruct((m, d), F32),
        compiler_params=_params(("parallel",)),
        name="merge_out_proj",
    )(ya, ob, z2, z2, x, p_a, p_b, w_out)


def _ffn_kernel(h_ref, g1_ref, w1_ref, w2_ref, g2_ref, o_ref, m_ref):
    j = pl.program_id(1)

    @pl.when(j == 0)
    def _():
        x = h_ref[...]
        ms = jnp.mean(x * x, axis=-1, keepdims=True)
        m_ref[...] = (x * lax.rsqrt(ms + RMS_EPS) * g1_ref[...]).astype(BF16)
        o_ref[...] = x

    f = jnp.dot(m_ref[...], w1_ref[...], preferred_element_type=F32)
    f = jnp.square(jnp.maximum(f, 0.0)).astype(BF16)
    o_ref[...] += jnp.dot(f, w2_ref[...], preferred_element_type=F32)

    @pl.when(j == pl.num_programs(1) - 1)
    def _():
        y = o_ref[...]
        ms = jnp.mean(y * y, axis=-1, keepdims=True)
        o_ref[...] = y * lax.rsqrt(ms + RMS_EPS) * g2_ref[...]


def _ffn(h, g1, w1, w2, g2, *, tm, tf):
    m, d = h.shape
    dff = w1.shape[1]
    return pl.pallas_call(
        _ffn_kernel,
        grid=(m // tm, dff // tf),
        in_specs=[pl.BlockSpec((tm, d), lambda i, j: (i, 0)),
                  pl.BlockSpec((1, d), lambda i, j: (0, 0)),
                  pl.BlockSpec((d, tf), lambda i, j: (0, j)),
                  pl.BlockSpec((tf, d), lambda i, j: (j, 0)),
                  pl.BlockSpec((1, d), lambda i, j: (0, 0))],
        out_specs=pl.BlockSpec((tm, d), lambda i, j: (i, 0)),
        out_shape=jax.ShapeDtypeStruct((m, d), F32),
        scratch_shapes=[pltpu.VMEM((tm, d), BF16)],
        compiler_params=_params(("parallel", "arbitrary")),
        name="ffn_residual_norm",
    )(h, g1, w1, w2, g2)


def _pad_cols(w, width):
    return jnp.pad(w, ((0, 0), (0, width - w.shape[1])))


def _pad_rows(w, height):
    return jnp.pad(w, ((0, height - w.shape[0]), (0, 0)))


def kernel(x, norm_mix_g, w_in, mu_shift, w0, w_up, a0, a_up, g_up, k_k, k_a, r_k, lnx_g, lnx_b, lambda_q1, lambda_k1, lambda_q2, lambda_k2, subln_g, rel_bias, p_a, p_b, w_out, norm_mlp_g, w_ff1, w_ff2, norm_final_g):
    b, s, d = x.shape
    depth = w_in.shape[0]
    a_width = w0.shape[1]
    d_lora, a_lora, g_lora = w_up.shape[1], a_up.shape[1], g_up.shape[1]
    n_bheads = rel_bias.shape[1] // 2
    b_width = n_bheads * LANES
    assert depth == 1, "the final norm is fused into the (single) layer's MLP kernel"
    assert d_lora <= LANES and a_lora <= LANES and g_lora == 2 * LANES
    assert a_width % LANES == 0 and s % 512 == 0 and subln_g.shape[1] == LANES
    m = b * s

    o_wd = 3 * a_width
    o_ad = o_wd + d_lora
    o_gd = o_ad + a_lora
    o_b = o_gd + g_lora
    o_g = o_b + 3 * b_width
    rkv_off = 2 * d
    lora_off = rkv_off + 3 * a_width
    bq_off = lora_off + 4 * LANES

    def regroup_cols(w):
        return jnp.concatenate([
            w[:, o_g:], w[:, :o_wd], _pad_cols(w[:, o_wd:o_ad], LANES), _pad_cols(w[:, o_ad:o_gd], LANES),
            w[:, o_gd:o_b], w[:, o_b:o_g]], axis=1)

    h = x.reshape(m, d)
    l = 0
    lambda_init = 0.8 - 0.6 * math.exp(-0.3 * l)
    w_in_r = regroup_cols(w_in[l]).astype(BF16)
    mu = mu_shift[l][None, :]
    mu_l = regroup_cols(jnp.pad(mu, ((0, 0), (0, w_in.shape[2] - mu.shape[1]))))[:, lora_off:bq_off]

    z2 = _norm_matmul(h, norm_mix_g[l][None, :], w_in_r, tm=min(1024, m), tn=1536, relu2=False,
                      out_dtype=BF16)
    z3 = z2.reshape(b, s, z2.shape[1])

    ya = _rwkv_branch(
        z3, mu[:, :a_width], mu[:, a_width:2 * a_width], mu[:, 2 * a_width:3 * a_width], mu_l,
        w0[l][None, :], _pad_rows(w_up[l], LANES).astype(BF16), a0[l][None, :],
        _pad_rows(a_up[l], LANES).astype(BF16), g_up[l].astype(BF16),
        k_k[l][None, :], k_a[l][None, :], r_k[l].reshape(1, a_width), lnx_g[l][None, :], lnx_b[l][None, :],
        a_width=a_width, rkv_block=rkv_off // a_width, lora_block=lora_off // (4 * LANES), ts=2 * CHUNK,
        out_dtype=BF16)

    ob = _attn_branch(
        z3, _bias_tiles(rel_bias, n_bheads), lambda_q1[l][None, :], lambda_k1[l][None, :],
        lambda_q2[l][None, :], lambda_k2[l][None, :], subln_g[l][None, :],
        q_block=bq_off // LANES, n_heads=n_bheads, lambda_init=lambda_init, out_dtype=BF16)

    h = _merge_out(ya.reshape(m, a_width), ob.reshape(m, b_width), z2, h, p_a[l].astype(BF16),
                   p_b[l].astype(BF16), w_out[l].astype(BF16), gate_block=0, tm=256)
    h = _ffn(h, norm_mlp_g[l][None, :], w_ff1[l].astype(BF16), w_ff2[l].astype(BF16),
             norm_final_g[None, :], tm=1024, tf=512)
    return h.reshape(b, s, d)
```

```python
import functools
import math

import jax
import jax.numpy as jnp
from jax import lax
from jax.experimental import pallas as pl
from jax.experimental.pallas import tpu as pltpu

F32 = jnp.float32
BF16 = jnp.bfloat16

LANES = 128
HEAD = 64
CHUNK = 64
Q_BLOCK = 128
MXU_DIM = 256
RMS_EPS = 1e-6
GN_EPS = 64e-5
SUBLN_EPS = 1e-5
N_BUCKETS = 32
MAX_DISTANCE = 128
NEG_INF = -1e30
LOG2E = 1.4426950408889634
VMEM_LIMIT = 56 * 1024 * 1024


def _mm(a, b):
    return jnp.dot(a.astype(BF16), b.astype(BF16), preferred_element_type=F32)


def _mm_nt(a, b):
    return lax.dot_general(a.astype(BF16), b.astype(BF16), (((1,), (1,)), ((), ())),
                           preferred_element_type=F32)


def _params(sem):
    return pltpu.CompilerParams(dimension_semantics=sem, vmem_limit_bytes=VMEM_LIMIT)


def _norm_matmul_kernel(x_ref, g_ref, w_ref, o_ref, u_ref, *, relu2):
    @pl.when(pl.program_id(1) == 0)
    def _():
        x = x_ref[...]
        ms = jnp.mean(x * x, axis=-1, keepdims=True)
        u_ref[...] = (x * lax.rsqrt(ms + RMS_EPS) * g_ref[...]).astype(BF16)

    acc = jnp.dot(u_ref[...], w_ref[...], preferred_element_type=F32)
    if relu2:
        acc = jnp.square(jnp.maximum(acc, 0.0))
    o_ref[...] = acc.astype(o_ref.dtype)


def _norm_matmul(x, g, w, *, tm, tn, relu2, out_dtype):
    m, d = x.shape
    n = w.shape[1]
    return pl.pallas_call(
        functools.partial(_norm_matmul_kernel, relu2=relu2),
        grid=(m // tm, n // tn),
        in_specs=[pl.BlockSpec((tm, d), lambda i, j: (i, 0)),
                  pl.BlockSpec((1, d), lambda i, j: (0, 0)),
                  pl.BlockSpec((d, tn), lambda i, j: (0, j))],
        out_specs=pl.BlockSpec((tm, tn), lambda i, j: (i, j)),
        out_shape=jax.ShapeDtypeStruct((m, n), out_dtype),
        scratch_shapes=[pltpu.VMEM((tm, d), BF16)],
        compiler_params=_params(("parallel", "arbitrary")),
        name="norm_matmul_relu2" if relu2 else "norm_matmul",
    )(x, g, w)


def _expand(t, lane_lo):
    zero = jnp.zeros_like(t)
    return jnp.concatenate([jnp.where(lane_lo, t, zero), jnp.where(lane_lo, zero, t)], axis=0)


def _sum_heads(x, head_ones):
    xb = x.astype(BF16)
    g = head_ones.shape[0]
    return jnp.concatenate(
        [jnp.dot(xb[:, i * g:(i + 1) * g], head_ones, preferred_element_type=F32)
         for i in range(x.shape[1] // g)], axis=1)


def _chunk_cumsum(x, tri):
    hi = x.astype(BF16)
    lo = (x - hi.astype(F32)).astype(BF16)
    out = jnp.dot(tri, jnp.concatenate([hi, lo], axis=1), preferred_element_type=F32)
    return out[:, :x.shape[1]] + out[:, x.shape[1]:]


def _rwkv_kernel(zr_ref, zk_ref, zv_ref, zl_ref, mur_ref, muk_ref, muv_ref, mul_ref,
                 w0_ref, wup_ref, a0_ref, aup_ref, gup_ref, kk_ref, ka_ref, rk_ref, lng_ref, lnb_ref,
                 o_ref, cr_ref, ck_ref, cv_ref, cl_ref, h_ref, *, n_chunks, n_pairs):
    c = CHUNK
    c2 = 2 * c
    ts = n_chunks * c
    units = [(p, ch) for ch in range(n_chunks) for p in range(n_pairs)]
    chunks = range(len(units))

    @pl.when(pl.program_id(1) == 0)
    def _():
        cr_ref[...] = jnp.zeros_like(cr_ref)
        ck_ref[...] = jnp.zeros_like(ck_ref)
        cv_ref[...] = jnp.zeros_like(cv_ref)
        cl_ref[...] = jnp.zeros_like(cl_ref)
        h_ref[...] = jnp.zeros_like(h_ref)

    def shifted(z, carry_ref, mu):
        rows = lax.broadcasted_iota(jnp.int32, z.shape, 0)
        prev = jnp.where(rows == 0, carry_ref[...], pltpu.roll(z, 1, 0))
        carry_ref[...] = z[z.shape[0] - 1:, :]
        return z + (prev - z) * mu

    ri = lax.broadcasted_iota(jnp.int32, (c2, c2), 0)
    ci = lax.broadcasted_iota(jnp.int32, (c2, c2), 1)
    strict = ci < ri
    incl = ci <= ri
    eye = (ci == ri).astype(F32)
    hr = lax.broadcasted_iota(jnp.int32, (MXU_DIM, MXU_DIM), 0)
    hc = lax.broadcasted_iota(jnp.int32, (MXU_DIM, MXU_DIM), 1)
    head_ones = ((hr // HEAD) == (hc // HEAD)).astype(BF16)
    ti = lax.broadcasted_iota(jnp.int32, (ts, ts), 0)
    tj = lax.broadcasted_iota(jnp.int32, (ts, ts), 1)
    tri = ((tj <= ti) & ((tj // c) == (ti // c))).astype(BF16)
    lane_lo = lax.broadcasted_iota(jnp.int32, (c, LANES), 1) < HEAD

    r = shifted(zr_ref[0].astype(F32), cr_ref, mur_ref[...])
    k = shifted(zk_ref[0].astype(F32), ck_ref, muk_ref[...])
    v = shifted(zv_ref[0].astype(F32), cv_ref, muv_ref[...])
    lo = shifted(zl_ref[0].astype(F32), cl_ref, mul_ref[...])

    t_w = w0_ref[...] + _mm(jnp.tanh(lo[:, 0:LANES]), wup_ref[...])
    lw = -math.exp(-0.5) * jax.nn.sigmoid(t_w)
    a_sig = jax.nn.sigmoid(a0_ref[...] + _mm(lo[:, LANES:2 * LANES], aup_ref[...]))
    gate = _mm(jax.nn.sigmoid(lo[:, 2 * LANES:]), gup_ref[...])
    kk = k * kk_ref[...]
    kk = kk * lax.rsqrt(jnp.maximum(_sum_heads(kk * kk, head_ones), 1e-24))
    k2 = k * (1.0 + (a_sig - 1.0) * ka_ref[...])
    b_v = kk * a_sig
    bonus = _sum_heads(r * k2 * rk_ref[...], head_ones) * v
    cum = _chunk_cumsum(lw, tri)
    g_in = jnp.exp(cum)
    g_inv = jnp.exp(-cum)
    rt_all = r * g_in
    at_all = -kk * jnp.exp(cum - lw)
    kt_all = k2 * g_inv
    bt_all = b_v * g_inv

    def tile(x, u):
        p, ch = units[u]
        return x[ch * c:(ch + 1) * c, p * LANES:(p + 1) * LANES]

    cum_end = [tile(cum, u)[c - 1:, :] for u in chunks]
    g_tail = [jnp.exp(cum_end[u] - tile(cum, u)) for u in chunks]
    rt = [_expand(tile(rt_all, u), lane_lo) for u in chunks]
    at = [_expand(tile(at_all, u), lane_lo) for u in chunks]
    kt = [_expand(tile(kt_all, u), lane_lo) for u in chunks]
    bt = [_expand(tile(bt_all, u), lane_lo) for u in chunks]
    kb = [_expand(tile(k2, u) * g_tail[u], lane_lo) for u in chunks]
    bb = [_expand(tile(b_v, u) * g_tail[u], lane_lo) for u in chunks]
    ve = [_expand(tile(v, u), lane_lo) for u in chunks]

    sc = [_mm_nt(jnp.concatenate([at[ch], rt[ch]], axis=0), jnp.concatenate([bt[ch], kt[ch]], axis=0))
          for ch in chunks]
    n_ab = [jnp.where(strict, sc[ch][:c2, :c2], 0.0) for ch in chunks]
    a_ak = [jnp.where(strict, sc[ch][:c2, c2:], 0.0) for ch in chunks]
    a_rb = [jnp.where(incl, sc[ch][c2:, :c2], 0.0) for ch in chunks]
    a_rk = [jnp.where(incl, sc[ch][c2:, c2:], 0.0) for ch in chunks]

    p = [eye + n_ab[ch] for ch in chunks]
    nk = [_mm(n_ab[ch], n_ab[ch]) for ch in chunks]
    akv = [_mm(a_ak[ch], ve[ch]) for ch in chunks]
    power = 2
    while power < c:
        if 2 * power < c:
            both = [_mm(nk[ch], jnp.concatenate([nk[ch], p[ch]], axis=1)) for ch in chunks]
            nk = [both[ch][:, :c2] for ch in chunks]
            p = [p[ch] + both[ch][:, c2:] for ch in chunks]
        else:
            p = [p[ch] + _mm(nk[ch], p[ch]) for ch in chunks]
        power *= 2

    ta = [_mm(p[ch], jnp.concatenate([at[ch], akv[ch]], axis=1)) for ch in chunks]
    zero = jnp.zeros((c2, LANES), F32)
    rhs = [jnp.concatenate([ta[ch], jnp.concatenate([zero, ve[ch]], axis=1)], axis=0) for ch in chunks]
    lhs = [jnp.concatenate([jnp.concatenate([a_rb[ch], a_rk[ch]], axis=1),
                            jnp.concatenate([bb[ch].T, kb[ch].T], axis=1)], axis=0) for ch in chunks]
    out = [_mm(lhs[ch], rhs[ch]) for ch in chunks]
    r_hat = [rt[ch] + out[ch][:c2, :LANES] for ch in chunks]
    y_in = [out[ch][:c2, LANES:] for ch in chunks]
    m_state = [eye * jnp.exp(cum_end[ch]) + out[ch][c2:, :LANES] for ch in chunks]
    g_state = [out[ch][c2:, LANES:] for ch in chunks]

    h = [h_ref[p] for p in range(n_pairs)]
    ys = [[] for _ in range(n_pairs)]
    for u in chunks:
        p = units[u][0]
        ye = _mm(r_hat[u], h[p]) + y_in[u]
        h[p] = _mm(m_state[u], h[p]) + g_state[u]
        ys[p].append(ye[:c] + ye[c:])
    for p in range(n_pairs):
        h_ref[p] = h[p]
    y = jnp.concatenate([jnp.concatenate(ys[p], axis=0) for p in range(n_pairs)], axis=1)

    mean = _sum_heads(y, head_ones) * (1.0 / HEAD)
    d = y - mean
    var = _sum_heads(d * d, head_ones) * (1.0 / HEAD)
    yn = d * lax.rsqrt(var + GN_EPS) * lng_ref[...] + lnb_ref[...]
    o_ref[0] = ((yn + bonus) * gate).astype(o_ref.dtype)


def _rwkv_branch(z3, mu_r, mu_k, mu_v, mu_l, w0, wup, a0, aup, gup, k_k, k_a, r_k, lnx_g, lnx_b,
                 *, a_width, rkv_block, lora_block, ts, out_dtype):
    b, s, _ = z3.shape
    n_pairs = a_width // LANES
    lw = 4 * LANES

    def col(blk):
        return lambda bi, si: (bi, si, blk)

    def full(rows, width):
        return pl.BlockSpec((rows, width), lambda bi, si: (0, 0))

    in_specs = [
        pl.BlockSpec((1, ts, a_width), col(rkv_block)),
        pl.BlockSpec((1, ts, a_width), col(rkv_block + 1)),
        pl.BlockSpec((1, ts, a_width), col(rkv_block + 2)),
        pl.BlockSpec((1, ts, lw), col(lora_block)),
        full(1, a_width), full(1, a_width), full(1, a_width), full(1, lw),
        full(1, a_width), full(LANES, a_width), full(1, a_width), full(LANES, a_width),
        full(2 * LANES, a_width),
        full(1, a_width), full(1, a_width), full(1, a_width), full(1, a_width), full(1, a_width),
    ]
    return pl.pallas_call(
        functools.partial(_rwkv_kernel, n_chunks=ts // CHUNK, n_pairs=n_pairs),
        grid=(b, s // ts),
        in_specs=in_specs,
        out_specs=pl.BlockSpec((1, ts, a_width), lambda bi, si: (bi, si, 0)),
        out_shape=jax.ShapeDtypeStruct((b, s, a_width), out_dtype),
        scratch_shapes=[pltpu.VMEM((1, a_width), F32), pltpu.VMEM((1, a_width), F32),
                        pltpu.VMEM((1, a_width), F32), pltpu.VMEM((1, lw), F32),
                        pltpu.VMEM((n_pairs, LANES, LANES), F32)],
        compiler_params=_params(("parallel", "arbitrary")),
        name="rwkv7_chunked",
    )(z3, z3, z3, z3, mu_r, mu_k, mu_v, mu_l, w0, wup, a0, aup, gup, k_k, k_a, r_k, lnx_g, lnx_b)


def _attn_kernel(q_ref, k_ref, v_ref, bias_ref, lq1_ref, lk1_ref, lq2_ref, lk2_ref, sg_ref, o_ref,
                 s_ref, p_ref, va_ref, *, n_qblocks, lambda_init):
    qr = Q_BLOCK
    lam = (jnp.exp(jnp.sum(lq1_ref[...] * lk1_ref[...])) - jnp.exp(jnp.sum(lq2_ref[...] * lk2_ref[...]))
           + lambda_init)
    lane_lo = lax.broadcasted_iota(jnp.int32, (qr, LANES), 1) < HEAD
    scale = HEAD ** -0.5 * LOG2E
    va_ref[:, :LANES] = v_ref[0].astype(BF16)
    va_ref[:, LANES:] = jnp.ones((va_ref.shape[0], LANES), BF16)

    def key_tiles(qb):
        return [(j, min(2, qb + 1 - j)) for j in range(0, qb + 1, 2)]

    def logits(qb):
        q = (q_ref[0, qb * qr:(qb + 1) * qr, :].astype(F32) * scale).astype(BF16)
        qe = _expand(q, lane_lo)
        m_l = None
        for j0, nb in key_tiles(qb):
            s = _mm_nt(qe, k_ref[0, j0 * qr:(j0 + nb) * qr, :])
            for t in range(nb):
                j = j0 + t
                sj = s[:, t * qr:(t + 1) * qr]
                if qb - j < 2:
                    sj = sj + bias_ref[0, qb - j]
                s_ref[qb % 2, :, j * qr:(j + 1) * qr] = sj
                m_l = sj if m_l is None else jnp.maximum(m_l, sj)
        return m_l

    def weighted_values(qb, m_l):
        m = jnp.max(m_l, axis=-1, keepdims=True)
        for j in range(qb + 1):
            pj = jnp.exp2(s_ref[qb % 2, :, j * qr:(j + 1) * qr] - m)
            p_ref[qb % 2, :, j * qr:(j + 1) * qr] = pj.astype(BF16)
        n_keys = (qb + 1) * qr
        acc = jnp.dot(p_ref[qb % 2, :, :n_keys], va_ref[:n_keys, :], preferred_element_type=F32)
        o = acc[:, :LANES] / acc[:, LANES:]
        o = o[:qr] - lam * o[qr:]
        o = o * lax.rsqrt(jnp.mean(o * o, axis=-1, keepdims=True) + SUBLN_EPS) * sg_ref[...]
        o = o * (1.0 - lambda_init)
        o_ref[0, qb * qr:(qb + 1) * qr, :] = o.astype(o_ref.dtype)

    pending = logits(0)
    for qb in range(n_qblocks):
        nxt = logits(qb + 1) if qb + 1 < n_qblocks else None
        weighted_values(qb, pending)
        pending = nxt


def _attn_branch(z3, bias, lq1, lk1, lq2, lk2, subln_g, *, q_block, n_heads, lambda_init, out_dtype):
    b, s, _ = z3.shape

    def col(off):
        return lambda bi, hh: (bi, 0, off + hh)

    def small(n):
        return pl.BlockSpec((1, n), lambda bi, hh: (0, 0))

    return pl.pallas_call(
        functools.partial(_attn_kernel, n_qblocks=s // Q_BLOCK, lambda_init=lambda_init),
        grid=(b, n_heads),
        in_specs=[pl.BlockSpec((1, s, LANES), col(q_block)),
                  pl.BlockSpec((1, s, LANES), col(q_block + n_heads)),
                  pl.BlockSpec((1, s, LANES), col(q_block + 2 * n_heads)),
                  pl.BlockSpec((1, 2, 2 * Q_BLOCK, Q_BLOCK), lambda bi, hh: (hh, 0, 0, 0)),
                  small(HEAD), small(HEAD), small(HEAD), small(HEAD), small(LANES)],
        out_specs=pl.BlockSpec((1, s, LANES), lambda bi, hh: (bi, 0, hh)),
        out_shape=jax.ShapeDtypeStruct((b, s, n_heads * LANES), out_dtype),
        scratch_shapes=[pltpu.VMEM((2, 2 * Q_BLOCK, s), F32), pltpu.VMEM((2, 2 * Q_BLOCK, s), BF16),
                        pltpu.VMEM((s, 2 * LANES), BF16)],
        compiler_params=_params(("parallel", "parallel")),
        name="diff_attention",
    )(z3, z3, z3, bias, lq1, lk1, lq2, lk2, subln_g)


def _t5_bucket(rel):
    nb = N_BUCKETS // 2
    max_exact = nb // 2
    ret = jnp.where(rel > 0, nb, 0)
    n = jnp.abs(rel)
    nf = jnp.maximum(n, 1).astype(jnp.float32)
    large = max_exact + (jnp.log(nf / max_exact) / math.log(MAX_DISTANCE / max_exact)
                         * (nb - max_exact)).astype(jnp.int32)
    large = jnp.minimum(large, nb - 1)
    return ret + jnp.where(n < max_exact, n, large)


def _bias_tiles(rel_bias, n_heads):
    assert Q_BLOCK >= MAX_DISTANCE
    dist = jnp.arange(3)[:, None, None]
    qi = jnp.arange(Q_BLOCK)[None, :, None]
    kj = jnp.arange(Q_BLOCK)[None, None, :]
    bucket = _t5_bucket(kj - qi - dist * Q_BLOCK)
    onehot = (bucket[..., None] == jnp.arange(N_BUCKETS)).astype(F32)
    t = jnp.einsum("dqkn,nh->hdqk", onehot, rel_bias.astype(F32), precision=lax.Precision.HIGHEST)
    allowed = (dist > 0) | ((kj // CHUNK) <= (qi // CHUNK))
    t = jnp.where(allowed[None], t, NEG_INF)
    t = (t[:, :2] - t[:, 2:]) * LOG2E
    t = t.reshape(n_heads, 2, 2, Q_BLOCK, Q_BLOCK).transpose(0, 2, 1, 3, 4)
    return t.reshape(n_heads, 2, 2 * Q_BLOCK, Q_BLOCK)


def _merge_out_kernel(ya_ref, ob_ref, ga_ref, gb_ref, x_ref, pa_ref, pb_ref, wo_ref, o_ref):
    oa = jnp.dot(ya_ref[...], pa_ref[...], preferred_element_type=F32)
    ob = jnp.dot(ob_ref[...], pb_ref[...], preferred_element_type=F32)
    merged = (jax.nn.sigmoid(ga_ref[...].astype(F32)) * oa
              + jax.nn.sigmoid(gb_ref[...].astype(F32)) * ob)
    o_ref[...] = x_ref[...] + jnp.dot(merged.astype(BF16), wo_ref[...], preferred_element_type=F32)


def _merge_out(ya, ob, z2, x, p_a, p_b, w_out, *, gate_block, tm):
    m, ka = ya.shape
    kb = ob.shape[1]
    d = w_out.shape[1]

    def resident(shape):
        return pl.BlockSpec(shape, lambda i: (0, 0), pipeline_mode=pl.Buffered(1))

    return pl.pallas_call(
        _merge_out_kernel,
        grid=(m // tm,),
        in_specs=[pl.BlockSpec((tm, ka), lambda i: (i, 0)),
                  pl.BlockSpec((tm, kb), lambda i: (i, 0)),
                  pl.BlockSpec((tm, d), lambda i: (i, gate_block)),
                  pl.BlockSpec((tm, d), lambda i: (i, gate_block + 1)),
                  pl.BlockSpec((tm, d), lambda i: (i, 0)),
                  resident(p_a.shape), resident(p_b.shape), resident(w_out.shape)],
        out_specs=pl.BlockSpec((tm, d), lambda i: (i, 0)),
        out_shape=jax.ShapeDtypeStruct((m, d), F32),
        compiler_params=_params(("parallel",)),
        name="merge_out_proj",
    )(ya, ob, z2, z2, x, p_a, p_b, w_out)


def _ffn_kernel(h_ref, g1_ref, w1_ref, w2_ref, g2_ref, o_ref, m_ref):
    j = pl.program_id(1)

    @pl.when(j == 0)
    def _():
        x = h_ref[...]
        ms = jnp.mean(x * x, axis=-1, keepdims=True)
        m_ref[...] = (x * lax.rsqrt(ms + RMS_EPS) * g1_ref[...]).astype(BF16)
        o_ref[...] = x

    f = jnp.dot(m_ref[...], w1_ref[...], preferred_element_type=F32)
    f = jnp.square(jnp.maximum(f, 0.0)).astype(BF16)
    o_ref[...] += jnp.dot(f, w2_ref[...], preferred_element_type=F32)

    @pl.when(j == pl.num_programs(1) - 1)
    def _():
        y = o_ref[...]
        ms = jnp.mean(y * y, axis=-1, keepdims=True)
        o_ref[...] = y * lax.rsqrt(ms + RMS_EPS) * g2_ref[...]


def _ffn(h, g1, w1, w2, g2, *, tm, tf):
    m, d = h.shape
    dff = w1.shape[1]
    return pl.pallas_call(
        _ffn_kernel,
        grid=(m // tm, dff // tf),
        in_specs=[pl.BlockSpec((tm, d), lambda i, j: (i, 0)),
                  pl.BlockSpec((1, d), lambda i, j: (0, 0)),
                  pl.BlockSpec((d, tf), lambda i, j: (0, j)),
                  pl.BlockSpec((tf, d), lambda i, j: (j, 0)),
                  pl.BlockSpec((1, d), lambda i, j: (0, 0))],
        out_specs=pl.BlockSpec((tm, d), lambda i, j: (i, 0)),
        out_shape=jax.ShapeDtypeStruct((m, d), F32),
        scratch_shapes=[pltpu.VMEM((tm, d), BF16)],
        compiler_params=_params(("parallel", "arbitrary")),
        name="ffn_residual_norm",
    )(h, g1, w1, w2, g2)


def _pad_cols(w, width):
    return jnp.pad(w, ((0, 0), (0, width - w.shape[1])))


def _pad_rows(w, height):
    return jnp.pad(w, ((0, height - w.shape[0]), (0, 0)))


def _tile_plan(m, s, z_cols, d_ff):
    plan = dict(
        inproj_tm=min(1024, m), inproj_tn=1536,
        rwkv_ts=4 * CHUNK,
        merge_tm=min(256, m),
        ffn_tm=min(512, m), ffn_tf=1024)
    assert m % plan["inproj_tm"] == 0 and z_cols % plan["inproj_tn"] == 0
    assert s % plan["rwkv_ts"] == 0 and s % (2 * Q_BLOCK) == 0
    assert m % plan["merge_tm"] == 0 and m % plan["ffn_tm"] == 0 and d_ff % plan["ffn_tf"] == 0
    return plan


def kernel(x, norm_mix_g, w_in, mu_shift, w0, w_up, a0, a_up, g_up, k_k, k_a, r_k, lnx_g, lnx_b, lambda_q1, lambda_k1, lambda_q2, lambda_k2, subln_g, rel_bias, p_a, p_b, w_out, norm_mlp_g, w_ff1, w_ff2, norm_final_g):
    b, s, d = x.shape
    depth = w_in.shape[0]
    a_width = w0.shape[1]
    d_lora, a_lora, g_lora = w_up.shape[1], a_up.shape[1], g_up.shape[1]
    n_bheads = rel_bias.shape[1] // 2
    b_width = n_bheads * LANES
    assert depth == 1, "the final norm is fused into the (single) layer's MLP kernel"
    assert d_lora <= LANES and a_lora <= LANES and g_lora == 2 * LANES
    assert a_width % MXU_DIM == 0 and subln_g.shape[1] == LANES
    m = b * s

    o_wd = 3 * a_width
    o_ad = o_wd + d_lora
    o_gd = o_ad + a_lora
    o_b = o_gd + g_lora
    o_g = o_b + 3 * b_width
    rkv_off = 2 * d
    lora_off = rkv_off + 3 * a_width
    bq_off = lora_off + 4 * LANES

    def regroup_cols(w):
        return jnp.concatenate([
            w[:, o_g:], w[:, :o_wd], _pad_cols(w[:, o_wd:o_ad], LANES), _pad_cols(w[:, o_ad:o_gd], LANES),
            w[:, o_gd:o_b], w[:, o_b:o_g]], axis=1)

    h = x.reshape(m, d)
    l = 0
    lambda_init = 0.8 - 0.6 * math.exp(-0.3 * l)
    w_in_r = regroup_cols(w_in[l].astype(BF16))
    mu = mu_shift[l][None, :]
    mu_l = regroup_cols(jnp.pad(mu, ((0, 0), (0, w_in.shape[2] - mu.shape[1]))))[:, lora_off:bq_off]
    plan = _tile_plan(m, s, w_in_r.shape[1], w_ff1.shape[2])

    z2 = _norm_matmul(h, norm_mix_g[l][None, :], w_in_r, tm=plan["inproj_tm"], tn=plan["inproj_tn"],
                      relu2=False, out_dtype=BF16)
    z3 = z2.reshape(b, s, z2.shape[1])

    ya = _rwkv_branch(
        z3, mu[:, :a_width], mu[:, a_width:2 * a_width], mu[:, 2 * a_width:3 * a_width], mu_l,
        w0[l][None, :], _pad_rows(w_up[l], LANES).astype(BF16), a0[l][None, :],
        _pad_rows(a_up[l], LANES).astype(BF16), g_up[l].astype(BF16),
        k_k[l][None, :], k_a[l][None, :], r_k[l].reshape(1, a_width), lnx_g[l][None, :], lnx_b[l][None, :],
        a_width=a_width, rkv_block=rkv_off // a_width, lora_block=lora_off // (4 * LANES),
        ts=plan["rwkv_ts"], out_dtype=BF16)

    ob = _attn_branch(
        z3, _bias_tiles(rel_bias, n_bheads), lambda_q1[l][None, :], lambda_k1[l][None, :],
        lambda_q2[l][None, :], lambda_k2[l][None, :], subln_g[l][None, :],
        q_block=bq_off // LANES, n_heads=n_bheads, lambda_init=lambda_init, out_dtype=BF16)

    h = _merge_out(ya.reshape(m, a_width), ob.reshape(m, b_width), z2, h, p_a[l].astype(BF16),
                   p_b[l].astype(BF16), w_out[l].astype(BF16), gate_block=0, tm=plan["merge_tm"])
    h = _ffn(h, norm_mlp_g[l][None, :], w_ff1[l].astype(BF16), w_ff2[l].astype(BF16),
             norm_final_g[None, :], tm=plan["ffn_tm"], tf=plan["ffn_tf"])
    return h.reshape(b, s, d)
```

```python
import functools
import math

import jax
import jax.numpy as jnp
from jax import lax
from jax.experimental import pallas as pl
from jax.experimental.pallas import tpu as pltpu

F32 = jnp.float32
BF16 = jnp.bfloat16

LANES = 128
HEAD = 64
CHUNK = 64
Q_BLOCK = 128
MXU_DIM = 256
RMS_EPS = 1e-6
GN_EPS = 64e-5
SUBLN_EPS = 1e-5
N_BUCKETS = 32
MAX_DISTANCE = 128
NEG_INF = -1e30
LOG2E = 1.4426950408889634
VMEM_LIMIT = 56 * 1024 * 1024


def _mm(a, b):
    return jnp.dot(a.astype(BF16), b.astype(BF16), preferred_element_type=F32)


def _mm_nt(a, b):
    return lax.dot_general(a.astype(BF16), b.astype(BF16), (((1,), (1,)), ((), ())),
                           preferred_element_type=F32)


def _params(sem):
    return pltpu.CompilerParams(dimension_semantics=sem, vmem_limit_bytes=VMEM_LIMIT)


def _norm_matmul_kernel(x_ref, g_ref, w_ref, o_ref, u_ref, *, relu2):
    @pl.when(pl.program_id(1) == 0)
    def _():
        x = x_ref[...]
        ms = jnp.mean(x * x, axis=-1, keepdims=True)
        u_ref[...] = (x * lax.rsqrt(ms + RMS_EPS) * g_ref[...]).astype(BF16)

    acc = jnp.dot(u_ref[...], w_ref[...], preferred_element_type=F32)
    if relu2:
        acc = jnp.square(jnp.maximum(acc, 0.0))
    o_ref[...] = acc.astype(o_ref.dtype)


def _norm_matmul(x, g, w, *, tm, tn, relu2, out_dtype):
    m, d = x.shape
    n = w.shape[1]
    return pl.pallas_call(
        functools.partial(_norm_matmul_kernel, relu2=relu2),
        grid=(m // tm, n // tn),
        in_specs=[pl.BlockSpec((tm, d), lambda i, j: (i, 0)),
                  pl.BlockSpec((1, d), lambda i, j: (0, 0)),
                  pl.BlockSpec((d, tn), lambda i, j: (0, j))],
        out_specs=pl.BlockSpec((tm, tn), lambda i, j: (i, j)),
        out_shape=jax.ShapeDtypeStruct((m, n), out_dtype),
        scratch_shapes=[pltpu.VMEM((tm, d), BF16)],
        compiler_params=_params(("parallel", "arbitrary")),
        name="norm_matmul_relu2" if relu2 else "norm_matmul",
    )(x, g, w)


def _expand(t, lane_lo):
    zero = jnp.zeros_like(t)
    return jnp.concatenate([jnp.where(lane_lo, t, zero), jnp.where(lane_lo, zero, t)], axis=0)


def _sum_heads(x, head_ones):
    xb = x.astype(BF16)
    g = head_ones.shape[0]
    return jnp.concatenate(
        [jnp.dot(xb[:, i * g:(i + 1) * g], head_ones, preferred_element_type=F32)
         for i in range(x.shape[1] // g)], axis=1)


def _chunk_cumsum(x, tri):
    hi = x.astype(BF16)
    lo = (x - hi.astype(F32)).astype(BF16)
    out = jnp.dot(tri, jnp.concatenate([hi, lo], axis=1), preferred_element_type=F32)
    return out[:, :x.shape[1]] + out[:, x.shape[1]:]


def _rwkv_kernel(zr_ref, zk_ref, zv_ref, zl_ref, mur_ref, muk_ref, muv_ref, mul_ref,
                 w0_ref, wup_ref, a0_ref, aup_ref, gup_ref, kk_ref, ka_ref, rk_ref, lng_ref, lnb_ref,
                 o_ref, cr_ref, ck_ref, cv_ref, cl_ref, h_ref, *, n_chunks, n_pairs, windows):
    c = CHUNK
    c2 = 2 * c
    ts = n_chunks * c
    units = [(p, ch) for ch in range(n_chunks) for p in range(n_pairs)]
    chunks = range(len(units))

    @pl.when(pl.program_id(1) == 0)
    def _():
        cr_ref[...] = jnp.zeros_like(cr_ref)
        ck_ref[...] = jnp.zeros_like(ck_ref)
        cv_ref[...] = jnp.zeros_like(cv_ref)
        cl_ref[...] = jnp.zeros_like(cl_ref)
        h_ref[...] = jnp.zeros_like(h_ref)

    def shifted(z, carry_ref, mu):
        rows = lax.broadcasted_iota(jnp.int32, z.shape, 0)
        prev = jnp.where(rows == 0, carry_ref[...], pltpu.roll(z, 1, 0))
        carry_ref[...] = z[z.shape[0] - 1:, :]
        return z + (prev - z) * mu

    ri = lax.broadcasted_iota(jnp.int32, (c2, c2), 0)
    ci = lax.broadcasted_iota(jnp.int32, (c2, c2), 1)
    strict = ci < ri
    incl = ci <= ri
    eye = (ci == ri).astype(F32)
    hr = lax.broadcasted_iota(jnp.int32, (MXU_DIM, MXU_DIM), 0)
    hc = lax.broadcasted_iota(jnp.int32, (MXU_DIM, MXU_DIM), 1)
    head_ones = ((hr // HEAD) == (hc // HEAD)).astype(BF16)
    ti = lax.broadcasted_iota(jnp.int32, (ts, ts), 0)
    tj = lax.broadcasted_iota(jnp.int32, (ts, ts), 1)
    tri = ((tj <= ti) & ((tj // c) == (ti // c))).astype(BF16)
    lane_lo = lax.broadcasted_iota(jnp.int32, (c, LANES), 1) < HEAD

    r = shifted(zr_ref[0].astype(F32), cr_ref, mur_ref[...])
    k = shifted(zk_ref[0].astype(F32), ck_ref, muk_ref[...])
    v = shifted(zv_ref[0].astype(F32), cv_ref, muv_ref[...])
    lo = shifted(zl_ref[0].astype(F32), cl_ref, mul_ref[...])

    (d0, d1), (a0_, a1_), (g0, g1) = windows
    t_w = w0_ref[...] + _mm(jnp.tanh(lo[:, d0:d1]), wup_ref[...])
    lw = -math.exp(-0.5) * jax.nn.sigmoid(t_w)
    a_sig = jax.nn.sigmoid(a0_ref[...] + _mm(lo[:, a0_:a1_], aup_ref[...]))
    gate = _mm(jax.nn.sigmoid(lo[:, g0:g1]), gup_ref[...])
    kk = k * kk_ref[...]
    kk = kk * lax.rsqrt(jnp.maximum(_sum_heads(kk * kk, head_ones), 1e-24))
    k2 = k * (1.0 + (a_sig - 1.0) * ka_ref[...])
    b_v = kk * a_sig
    bonus = _sum_heads(r * k2 * rk_ref[...], head_ones) * v
    cum = _chunk_cumsum(lw, tri)
    g_in = jnp.exp(cum)
    g_inv = jnp.exp(-cum)
    rt_all = r * g_in
    at_all = -kk * jnp.exp(cum - lw)
    kt_all = k2 * g_inv
    bt_all = b_v * g_inv

    def tile(x, u):
        p, ch = units[u]
        return x[ch * c:(ch + 1) * c, p * LANES:(p + 1) * LANES]

    cum_end = [tile(cum, u)[c - 1:, :] for u in chunks]
    g_tail = [jnp.exp(cum_end[u] - tile(cum, u)) for u in chunks]
    rt = [_expand(tile(rt_all, u), lane_lo) for u in chunks]
    at = [_expand(tile(at_all, u), lane_lo) for u in chunks]
    kt = [_expand(tile(kt_all, u), lane_lo) for u in chunks]
    bt = [_expand(tile(bt_all, u), lane_lo) for u in chunks]
    kb = [_expand(tile(k2, u) * g_tail[u], lane_lo) for u in chunks]
    bb = [_expand(tile(b_v, u) * g_tail[u], lane_lo) for u in chunks]
    ve = [_expand(tile(v, u), lane_lo) for u in chunks]

    sc = [_mm_nt(jnp.concatenate([at[ch], rt[ch]], axis=0), jnp.concatenate([bt[ch], kt[ch]], axis=0))
          for ch in chunks]
    n_ab = [jnp.where(strict, sc[ch][:c2, :c2], 0.0) for ch in chunks]
    a_ak = [jnp.where(strict, sc[ch][:c2, c2:], 0.0) for ch in chunks]
    a_rb = [jnp.where(incl, sc[ch][c2:, :c2], 0.0) for ch in chunks]
    a_rk = [jnp.where(incl, sc[ch][c2:, c2:], 0.0) for ch in chunks]

    p = [eye + n_ab[ch] for ch in chunks]
    nk = [_mm(n_ab[ch], n_ab[ch]) for ch in chunks]
    akv = [_mm(a_ak[ch], ve[ch]) for ch in chunks]
    power = 2
    while power < c:
        if 2 * power < c:
            both = [_mm(nk[ch], jnp.concatenate([nk[ch], p[ch]], axis=1)) for ch in chunks]
            nk = [both[ch][:, :c2] for ch in chunks]
            p = [p[ch] + both[ch][:, c2:] for ch in chunks]
        else:
            p = [p[ch] + _mm(nk[ch], p[ch]) for ch in chunks]
        power *= 2

    ta = [_mm(p[ch], jnp.concatenate([at[ch], akv[ch]], axis=1)) for ch in chunks]
    zero = jnp.zeros((c2, LANES), F32)
    rhs = [jnp.concatenate([ta[ch], jnp.concatenate([zero, ve[ch]], axis=1)], axis=0) for ch in chunks]
    lhs = [jnp.concatenate([jnp.concatenate([a_rb[ch], a_rk[ch]], axis=1),
                            jnp.concatenate([bb[ch].T, kb[ch].T], axis=1)], axis=0) for ch in chunks]
    out = [_mm(lhs[ch], rhs[ch]) for ch in chunks]
    r_hat = [rt[ch] + out[ch][:c2, :LANES] for ch in chunks]
    y_in = [out[ch][:c2, LANES:] for ch in chunks]
    m_state = [eye * jnp.exp(cum_end[ch]) + out[ch][c2:, :LANES] for ch in chunks]
    g_state = [out[ch][c2:, LANES:] for ch in chunks]

    h = [h_ref[p] for p in range(n_pairs)]
    ys = [[] for _ in range(n_pairs)]
    for u in chunks:
        p = units[u][0]
        both = _mm(jnp.concatenate([r_hat[u], m_state[u]], axis=0), h[p])
        ye = both[:c2] + y_in[u]
        h[p] = both[c2:] + g_state[u]
        ys[p].append(ye[:c] + ye[c:])
    for p in range(n_pairs):
        h_ref[p] = h[p]
    y = jnp.concatenate([jnp.concatenate(ys[p], axis=0) for p in range(n_pairs)], axis=1)

    mean = _sum_heads(y, head_ones) * (1.0 / HEAD)
    d = y - mean
    var = _sum_heads(d * d, head_ones) * (1.0 / HEAD)
    yn = d * lax.rsqrt(var + GN_EPS) * lng_ref[...] + lnb_ref[...]
    o_ref[0] = ((yn + bonus) * gate).astype(o_ref.dtype)


def _rwkv_branch(z3, mu_r, mu_k, mu_v, mu_l, w0, wup, a0, aup, gup, k_k, k_a, r_k, lnx_g, lnx_b,
                 *, a_width, rkv_block, lora_block, windows, ts, out_dtype):
    b, s, _ = z3.shape
    n_pairs = a_width // LANES
    lw = mu_l.shape[1]

    def col(blk):
        return lambda bi, si: (bi, si, blk)

    def full(rows, width):
        return pl.BlockSpec((rows, width), lambda bi, si: (0, 0))

    in_specs = [
        pl.BlockSpec((1, ts, a_width), col(rkv_block)),
        pl.BlockSpec((1, ts, a_width), col(rkv_block + 1)),
        pl.BlockSpec((1, ts, a_width), col(rkv_block + 2)),
        pl.BlockSpec((1, ts, lw), col(lora_block)),
        full(1, a_width), full(1, a_width), full(1, a_width), full(1, lw),
        full(1, a_width), full(wup.shape[0], a_width), full(1, a_width), full(aup.shape[0], a_width),
        full(gup.shape[0], a_width),
        full(1, a_width), full(1, a_width), full(1, a_width), full(1, a_width), full(1, a_width),
    ]
    return pl.pallas_call(
        functools.partial(_rwkv_kernel, n_chunks=ts // CHUNK, n_pairs=n_pairs, windows=windows),
        grid=(b, s // ts),
        in_specs=in_specs,
        out_specs=pl.BlockSpec((1, ts, a_width), lambda bi, si: (bi, si, 0)),
        out_shape=jax.ShapeDtypeStruct((b, s, a_width), out_dtype),
        scratch_shapes=[pltpu.VMEM((1, a_width), F32), pltpu.VMEM((1, a_width), F32),
                        pltpu.VMEM((1, a_width), F32), pltpu.VMEM((1, lw), F32),
                        pltpu.VMEM((n_pairs, LANES, LANES), F32)],
        compiler_params=_params(("parallel", "arbitrary")),
        name="rwkv7_chunked",
    )(z3, z3, z3, z3, mu_r, mu_k, mu_v, mu_l, w0, wup, a0, aup, gup, k_k, k_a, r_k, lnx_g, lnx_b)


def _attn_kernel(q_ref, k_ref, v_ref, bias_ref, lq1_ref, lk1_ref, lq2_ref, lk2_ref, sg_ref, o_ref,
                 s_ref, p_ref, va_ref, *, n_qblocks, lambda_init):
    qr = Q_BLOCK
    lam = (jnp.exp(jnp.sum(lq1_ref[...] * lk1_ref[...])) - jnp.exp(jnp.sum(lq2_ref[...] * lk2_ref[...]))
           + lambda_init)
    lane_lo = lax.broadcasted_iota(jnp.int32, (qr, LANES), 1) < HEAD
    scale = HEAD ** -0.5 * LOG2E

    def key_tiles(qb):
        return [(j, min(2, qb + 1 - j)) for j in range(0, qb + 1, 2)]

    def logits(qb):
        rows = slice(qb * qr, (qb + 1) * qr)
        va_ref[rows, :LANES] = v_ref[0, rows, :].astype(BF16)
        va_ref[rows, LANES:] = jnp.ones((qr, LANES), BF16)
        q = (q_ref[0, rows, :].astype(F32) * scale).astype(BF16)
        qe = _expand(q, lane_lo)
        m_l = None
        for j0, nb in key_tiles(qb):
            s = _mm_nt(qe, k_ref[0, j0 * qr:(j0 + nb) * qr, :])
            for t in range(nb):
                j = j0 + t
                sj = s[:, t * qr:(t + 1) * qr]
                if qb - j < 2:
                    sj = sj + bias_ref[0, qb - j]
                s_ref[qb % 2, :, j * qr:(j + 1) * qr] = sj
                m_l = sj if m_l is None else jnp.maximum(m_l, sj)
        return m_l

    def weighted_values(qb, m_l):
        m = jnp.max(m_l, axis=-1, keepdims=True)
        for j in range(qb + 1):
            pj = jnp.exp2(s_ref[qb % 2, :, j * qr:(j + 1) * qr] - m)
            p_ref[qb % 2, :, j * qr:(j + 1) * qr] = pj.astype(BF16)
        n_keys = (qb + 1) * qr
        acc = jnp.dot(p_ref[qb % 2, :, :n_keys], va_ref[:n_keys, :], preferred_element_type=F32)
        o = acc[:, :LANES] / acc[:, LANES:]
        o = o[:qr] - lam * o[qr:]
        o = o * lax.rsqrt(jnp.mean(o * o, axis=-1, keepdims=True) + SUBLN_EPS) * sg_ref[...]
        o = o * (1.0 - lambda_init)
        o_ref[0, qb * qr:(qb + 1) * qr, :] = o.astype(o_ref.dtype)

    pending = logits(0)
    for qb in range(n_qblocks):
        nxt = logits(qb + 1) if qb + 1 < n_qblocks else None
        weighted_values(qb, pending)
        pending = nxt


def _attn_branch(z3, bias, lq1, lk1, lq2, lk2, subln_g, *, q_block, n_heads, lambda_init, out_dtype):
    b, s, _ = z3.shape

    def col(off):
        return lambda bi, hh: (bi, 0, off + hh)

    def small(n):
        return pl.BlockSpec((1, n), lambda bi, hh: (0, 0))

    return pl.pallas_call(
        functools.partial(_attn_kernel, n_qblocks=s // Q_BLOCK, lambda_init=lambda_init),
        grid=(b, n_heads),
        in_specs=[pl.BlockSpec((1, s, LANES), col(q_block)),
                  pl.BlockSpec((1, s, LANES), col(q_block + n_heads)),
                  pl.BlockSpec((1, s, LANES), col(q_block + 2 * n_heads)),
                  pl.BlockSpec((1, 2, 2 * Q_BLOCK, Q_BLOCK), lambda bi, hh: (hh, 0, 0, 0)),
                  small(HEAD), small(HEAD), small(HEAD), small(HEAD), small(LANES)],
        out_specs=pl.BlockSpec((1, s, LANES), lambda bi, hh: (bi, 0, hh)),
        out_shape=jax.ShapeDtypeStruct((b, s, n_heads * LANES), out_dtype),
        scratch_shapes=[pltpu.VMEM((2, 2 * Q_BLOCK, s), F32), pltpu.VMEM((2, 2 * Q_BLOCK, s), BF16),
                        pltpu.VMEM((s, 2 * LANES), BF16)],
        compiler_params=_params(("parallel", "parallel")),
        name="diff_attention",
    )(z3, z3, z3, bias, lq1, lk1, lq2, lk2, subln_g)


def _t5_bucket(rel):
    nb = N_BUCKETS // 2
    max_exact = nb // 2
    ret = jnp.where(rel > 0, nb, 0)
    n = jnp.abs(rel)
    nf = jnp.maximum(n, 1).astype(jnp.float32)
    large = max_exact + (jnp.log(nf / max_exact) / math.log(MAX_DISTANCE / max_exact)
                         * (nb - max_exact)).astype(jnp.int32)
    large = jnp.minimum(large, nb - 1)
    return ret + jnp.where(n < max_exact, n, large)


def _bias_tiles(rel_bias, n_heads):
    assert Q_BLOCK >= MAX_DISTANCE
    dist = jnp.arange(3)[:, None, None]
    qi = jnp.arange(Q_BLOCK)[None, :, None]
    kj = jnp.arange(Q_BLOCK)[None, None, :]
    bucket = _t5_bucket(kj - qi - dist * Q_BLOCK)
    onehot = (bucket[..., None] == jnp.arange(N_BUCKETS)).astype(F32)
    t = jnp.einsum("dqkn,nh->hdqk", onehot, rel_bias.astype(F32), precision=lax.Precision.HIGHEST)
    allowed = (dist > 0) | ((kj // CHUNK) <= (qi // CHUNK))
    t = jnp.where(allowed[None], t, NEG_INF)
    t = (t[:, :2] - t[:, 2:]) * LOG2E
    t = t.reshape(n_heads, 2, 2, Q_BLOCK, Q_BLOCK).transpose(0, 2, 1, 3, 4)
    return t.reshape(n_heads, 2, 2 * Q_BLOCK, Q_BLOCK)


def _merge_out_kernel(ya_ref, ob_ref, ga_ref, gb_ref, x_ref, pa_ref, pb_ref, wo_ref, o_ref, mg_ref):
    half = mg_ref.shape[1] // 2
    for c0 in (0, half):
        cols = slice(c0, c0 + half)
        oa = jnp.dot(ya_ref[...], pa_ref[:, cols], preferred_element_type=F32)
        ob = jnp.dot(ob_ref[...], pb_ref[:, cols], preferred_element_type=F32)
        mg_ref[:, cols] = (jax.nn.sigmoid(ga_ref[:, cols].astype(F32)) * oa
                           + jax.nn.sigmoid(gb_ref[:, cols].astype(F32)) * ob).astype(BF16)
    o_ref[...] = x_ref[...] + jnp.dot(mg_ref[...], wo_ref[...], preferred_element_type=F32)


def _merge_out(ya, ob, z2, x, p_a, p_b, w_out, *, gate_block, tm):
    m, ka = ya.shape
    kb = ob.shape[1]
    d = w_out.shape[1]

    def resident(shape):
        return pl.BlockSpec(shape, lambda i: (0, 0), pipeline_mode=pl.Buffered(1))

    return pl.pallas_call(
        _merge_out_kernel,
        grid=(m // tm,),
        in_specs=[pl.BlockSpec((tm, ka), lambda i: (i, 0)),
                  pl.BlockSpec((tm, kb), lambda i: (i, 0)),
                  pl.BlockSpec((tm, d), lambda i: (i, gate_block)),
                  pl.BlockSpec((tm, d), lambda i: (i, gate_block + 1)),
                  pl.BlockSpec((tm, d), lambda i: (i, 0)),
                  resident(p_a.shape), resident(p_b.shape), resident(w_out.shape)],
        out_specs=pl.BlockSpec((tm, d), lambda i: (i, 0)),
        out_shape=jax.ShapeDtypeStruct((m, d), F32),
        scratch_shapes=[pltpu.VMEM((tm, d), BF16)],
        compiler_params=_params(("parallel",)),
        name="merge_out_proj",
    )(ya, ob, z2, z2, x, p_a, p_b, w_out)


def _ffn_kernel(h_ref, g1_ref, w1_ref, w2_ref, g2_ref, o_ref, m_ref):
    j = pl.program_id(1)

    @pl.when(j == 0)
    def _():
        x = h_ref[...]
        ms = jnp.mean(x * x, axis=-1, keepdims=True)
        m_ref[...] = (x * lax.rsqrt(ms + RMS_EPS) * g1_ref[...]).astype(BF16)
        o_ref[...] = x

    f = jnp.dot(m_ref[...], w1_ref[...], preferred_element_type=F32)
    f = jnp.square(jnp.maximum(f, 0.0)).astype(BF16)
    o_ref[...] += jnp.dot(f, w2_ref[...], preferred_element_type=F32)

    @pl.when(j == pl.num_programs(1) - 1)
    def _():
        y = o_ref[...]
        ms = jnp.mean(y * y, axis=-1, keepdims=True)
        o_ref[...] = y * lax.rsqrt(ms + RMS_EPS) * g2_ref[...]


def _ffn(h, g1, w1, w2, g2, *, tm, tf):
    m, d = h.shape
    dff = w1.shape[1]
    return pl.pallas_call(
        _ffn_kernel,
        grid=(m // tm, dff // tf),
        in_specs=[pl.BlockSpec((tm, d), lambda i, j: (i, 0)),
                  pl.BlockSpec((1, d), lambda i, j: (0, 0)),
                  pl.BlockSpec((d, tf), lambda i, j: (0, j)),
                  pl.BlockSpec((tf, d), lambda i, j: (j, 0)),
                  pl.BlockSpec((1, d), lambda i, j: (0, 0))],
        out_specs=pl.BlockSpec((tm, d), lambda i, j: (i, 0)),
        out_shape=jax.ShapeDtypeStruct((m, d), F32),
        scratch_shapes=[pltpu.VMEM((tm, d), BF16)],
        compiler_params=_params(("parallel", "arbitrary")),
        name="ffn_residual_norm",
    )(h, g1, w1, w2, g2)


def _pad_cols(w, width):
    return jnp.pad(w, ((0, 0), (0, width - w.shape[1])))


def _tile_plan(m, s, z_cols, d_ff):
    plan = dict(
        inproj_tm=min(1024, m), inproj_tn=1536,
        rwkv_ts=4 * CHUNK,
        merge_tm=min(512, m),
        ffn_tm=min(512, m), ffn_tf=1024)
    assert m % plan["inproj_tm"] == 0 and z_cols % plan["inproj_tn"] == 0
    assert s % plan["rwkv_ts"] == 0 and s % (2 * Q_BLOCK) == 0
    assert m % plan["merge_tm"] == 0 and m % plan["ffn_tm"] == 0 and d_ff % plan["ffn_tf"] == 0
    return plan


def kernel(x, norm_mix_g, w_in, mu_shift, w0, w_up, a0, a_up, g_up, k_k, k_a, r_k, lnx_g, lnx_b, lambda_q1, lambda_k1, lambda_q2, lambda_k2, subln_g, rel_bias, p_a, p_b, w_out, norm_mlp_g, w_ff1, w_ff2, norm_final_g):
    b, s, d = x.shape
    depth = w_in.shape[0]
    a_width = w0.shape[1]
    d_lora, a_lora, g_lora = w_up.shape[1], a_up.shape[1], g_up.shape[1]
    n_bheads = rel_bias.shape[1] // 2
    b_width = n_bheads * LANES
    assert depth == 1, "the final norm is fused into the (single) layer's MLP kernel"
    assert a_width % MXU_DIM == 0 and subln_g.shape[1] == LANES
    m = b * s

    o_wd = 3 * a_width
    o_b = o_wd + d_lora + a_lora + g_lora
    o_g = o_b + 3 * b_width
    lora_w = -(-(o_b - o_wd) // (4 * LANES)) * (4 * LANES)
    rkv_off = 2 * d
    lora_off = rkv_off + 3 * a_width
    bq_off = lora_off + lora_w
    assert lora_off % lora_w == 0

    def window(off, width):
        return (off // LANES * LANES, -(-(off + width) // LANES) * LANES)

    def embed_rows(w, off, win):
        return jnp.pad(w, ((off - win[0], win[1] - off - w.shape[0]), (0, 0))).astype(BF16)

    windows = (window(0, d_lora), window(d_lora, a_lora), window(d_lora + a_lora, g_lora))

    h = x.reshape(m, d)
    l = 0
    lambda_init = 0.8 - 0.6 * math.exp(-0.3 * l)
    w_bf = w_in[l].astype(BF16)
    w_in_r = jnp.concatenate([w_bf[:, o_g:], w_bf[:, :o_b], jnp.zeros((d, lora_w - (o_b - o_wd)), BF16),
                              w_bf[:, o_b:o_g]], axis=1)
    mu = mu_shift[l][None, :]
    mu_l = _pad_cols(mu[:, o_wd:o_b], lora_w)
    plan = _tile_plan(m, s, w_in_r.shape[1], w_ff1.shape[2])

    z2 = _norm_matmul(h, norm_mix_g[l][None, :], w_in_r, tm=plan["inproj_tm"], tn=plan["inproj_tn"],
                      relu2=False, out_dtype=BF16)
    z3 = z2.reshape(b, s, z2.shape[1])

    ya = _rwkv_branch(
        z3, mu[:, :a_width], mu[:, a_width:2 * a_width], mu[:, 2 * a_width:3 * a_width], mu_l,
        w0[l][None, :], embed_rows(w_up[l], 0, windows[0]), a0[l][None, :],
        embed_rows(a_up[l], d_lora, windows[1]), embed_rows(g_up[l], d_lora + a_lora, windows[2]),
        k_k[l][None, :], k_a[l][None, :], r_k[l].reshape(1, a_width), lnx_g[l][None, :], lnx_b[l][None, :],
        a_width=a_width, rkv_block=rkv_off // a_width, lora_block=lora_off // lora_w, windows=windows,
        ts=plan["rwkv_ts"], out_dtype=BF16)

    ob = _attn_branch(
        z3, _bias_tiles(rel_bias, n_bheads), lambda_q1[l][None, :], lambda_k1[l][None, :],
        lambda_q2[l][None, :], lambda_k2[l][None, :], subln_g[l][None, :],
        q_block=bq_off // LANES, n_heads=n_bheads, lambda_init=lambda_init, out_dtype=BF16)

    h = _merge_out(ya.reshape(m, a_width), ob.reshape(m, b_width), z2, h, p_a[l].astype(BF16),
                   p_b[l].astype(BF16), w_out[l].astype(BF16), gate_block=0, tm=plan["merge_tm"])
    h = _ffn(h, norm_mlp_g[l][None, :], w_ff1[l].astype(BF16), w_ff2[l].astype(BF16),
             norm_final_g[None, :], tm=plan["ffn_tm"], tf=plan["ffn_tf"])
    return h.reshape(b, s, d)
```

```python
import functools
import math

import jax
import jax.numpy as jnp
from jax import lax
from jax.experimental import pallas as pl
from jax.experimental.pallas import tpu as pltpu

F32 = jnp.float32
BF16 = jnp.bfloat16

LANES = 128
HEAD = 64
CHUNK = 64
Q_BLOCK = 128
MXU_DIM = 256
RMS_EPS = 1e-6
GN_EPS = 64e-5
SUBLN_EPS = 1e-5
N_BUCKETS = 32
MAX_DISTANCE = 128
NEG_INF = -1e30
LOG2E = 1.4426950408889634
VMEM_LIMIT = 56 * 1024 * 1024


def _mm(a, b):
    return jnp.dot(a.astype(BF16), b.astype(BF16), preferred_element_type=F32)


def _mm_nt(a, b):
    return lax.dot_general(a.astype(BF16), b.astype(BF16), (((1,), (1,)), ((), ())),
                           preferred_element_type=F32)


def _params(sem):
    return pltpu.CompilerParams(dimension_semantics=sem, vmem_limit_bytes=VMEM_LIMIT)


def _norm_matmul_kernel(x_ref, g_ref, w_ref, o_ref, u_ref, *, relu2):
    @pl.when(pl.program_id(1) == 0)
    def _():
        x = x_ref[...]
        ms = jnp.mean(x * x, axis=-1, keepdims=True)
        u_ref[...] = (x * lax.rsqrt(ms + RMS_EPS) * g_ref[...]).astype(BF16)

    acc = jnp.dot(u_ref[...], w_ref[...], preferred_element_type=F32)
    if relu2:
        acc = jnp.square(jnp.maximum(acc, 0.0))
    o_ref[...] = acc.astype(o_ref.dtype)


def _norm_matmul(x, g, w, *, tm, tn, relu2, out_dtype):
    m, d = x.shape
    n = w.shape[1]
    return pl.pallas_call(
        functools.partial(_norm_matmul_kernel, relu2=relu2),
        grid=(m // tm, n // tn),
        in_specs=[pl.BlockSpec((tm, d), lambda i, j: (i, 0)),
                  pl.BlockSpec((1, d), lambda i, j: (0, 0)),
                  pl.BlockSpec((d, tn), lambda i, j: (0, j))],
        out_specs=pl.BlockSpec((tm, tn), lambda i, j: (i, j)),
        out_shape=jax.ShapeDtypeStruct((m, n), out_dtype),
        scratch_shapes=[pltpu.VMEM((tm, d), BF16)],
        compiler_params=_params(("parallel", "arbitrary")),
        name="norm_matmul_relu2" if relu2 else "norm_matmul",
    )(x, g, w)


def _expand(t, lane_lo):
    zero = jnp.zeros_like(t)
    return jnp.concatenate([jnp.where(lane_lo, t, zero), jnp.where(lane_lo, zero, t)], axis=0)


def _sum_heads(x, head_ones):
    xb = x.astype(BF16)
    g = head_ones.shape[0]
    return jnp.concatenate(
        [jnp.dot(xb[:, i * g:(i + 1) * g], head_ones, preferred_element_type=F32)
         for i in range(x.shape[1] // g)], axis=1)


def _chunk_cumsum(x, tri):
    hi = x.astype(BF16)
    lo = (x - hi.astype(F32)).astype(BF16)
    out = jnp.dot(tri, jnp.concatenate([hi, lo], axis=1), preferred_element_type=F32)
    return out[:, :x.shape[1]] + out[:, x.shape[1]:]


def _rwkv_kernel(zr_ref, zk_ref, zv_ref, zl_ref, mur_ref, muk_ref, muv_ref, mul_ref,
                 w0_ref, wup_ref, a0_ref, aup_ref, gup_ref, kk_ref, ka_ref, rk_ref, lng_ref, lnb_ref,
                 o_ref, cr_ref, ck_ref, cv_ref, cl_ref, h_ref, *, n_chunks, n_pairs, windows):
    c = CHUNK
    c2 = 2 * c
    ts = n_chunks * c
    units = [(p, ch) for ch in range(n_chunks) for p in range(n_pairs)]
    chunks = range(len(units))

    @pl.when(pl.program_id(1) == 0)
    def _():
        cr_ref[...] = jnp.zeros_like(cr_ref)
        ck_ref[...] = jnp.zeros_like(ck_ref)
        cv_ref[...] = jnp.zeros_like(cv_ref)
        cl_ref[...] = jnp.zeros_like(cl_ref)
        h_ref[...] = jnp.zeros_like(h_ref)

    def shifted(z, carry_ref, mu):
        rows = lax.broadcasted_iota(jnp.int32, z.shape, 0)
        prev = jnp.where(rows == 0, carry_ref[...], pltpu.roll(z, 1, 0))
        carry_ref[...] = z[z.shape[0] - 1:, :]
        return z + (prev - z) * mu

    ri = lax.broadcasted_iota(jnp.int32, (c2, c2), 0)
    ci = lax.broadcasted_iota(jnp.int32, (c2, c2), 1)
    strict = ci < ri
    incl = ci <= ri
    eye = (ci == ri).astype(F32)
    hr = lax.broadcasted_iota(jnp.int32, (MXU_DIM, MXU_DIM), 0)
    hc = lax.broadcasted_iota(jnp.int32, (MXU_DIM, MXU_DIM), 1)
    head_ones = ((hr // HEAD) == (hc // HEAD)).astype(BF16)
    ti = lax.broadcasted_iota(jnp.int32, (ts, ts), 0)
    tj = lax.broadcasted_iota(jnp.int32, (ts, ts), 1)
    tri = ((tj <= ti) & ((tj // c) == (ti // c))).astype(BF16)
    lane_lo = lax.broadcasted_iota(jnp.int32, (c, LANES), 1) < HEAD

    r = shifted(zr_ref[0].astype(F32), cr_ref, mur_ref[...])
    k = shifted(zk_ref[0].astype(F32), ck_ref, muk_ref[...])
    v = shifted(zv_ref[0].astype(F32), cv_ref, muv_ref[...])
    lo = shifted(zl_ref[0].astype(F32), cl_ref, mul_ref[...])

    (d0, d1), (a0_, a1_), (g0, g1) = windows
    t_w = w0_ref[...] + _mm(jnp.tanh(lo[:, d0:d1]), wup_ref[...])
    lw = -math.exp(-0.5) * jax.nn.sigmoid(t_w)
    a_sig = jax.nn.sigmoid(a0_ref[...] + _mm(lo[:, a0_:a1_], aup_ref[...]))
    gate = _mm(jax.nn.sigmoid(lo[:, g0:g1]), gup_ref[...])
    kk = k * kk_ref[...]
    kk = kk * lax.rsqrt(jnp.maximum(_sum_heads(kk * kk, head_ones), 1e-24))
    k2 = k * (1.0 + (a_sig - 1.0) * ka_ref[...])
    b_v = kk * a_sig
    bonus = _sum_heads(r * k2 * rk_ref[...], head_ones) * v
    cum = _chunk_cumsum(lw, tri)
    g_in = jnp.exp(cum)
    g_inv = jnp.exp(-cum)
    rt_all = r * g_in
    at_all = -kk * jnp.exp(cum - lw)
    kt_all = k2 * g_inv
    bt_all = b_v * g_inv

    def tile(x, u):
        p, ch = units[u]
        return x[ch * c:(ch + 1) * c, p * LANES:(p + 1) * LANES]

    cum_end = [tile(cum, u)[c - 1:, :] for u in chunks]
    g_tail = [jnp.exp(cum_end[u] - tile(cum, u)) for u in chunks]
    rt = [_expand(tile(rt_all, u), lane_lo) for u in chunks]
    at = [_expand(tile(at_all, u), lane_lo) for u in chunks]
    kt = [_expand(tile(kt_all, u), lane_lo) for u in chunks]
    bt = [_expand(tile(bt_all, u), lane_lo) for u in chunks]
    kb = [_expand(tile(k2, u) * g_tail[u], lane_lo) for u in chunks]
    bb = [_expand(tile(b_v, u) * g_tail[u], lane_lo) for u in chunks]
    ve = [_expand(tile(v, u), lane_lo) for u in chunks]

    sc = [_mm_nt(jnp.concatenate([at[ch], rt[ch]], axis=0), jnp.concatenate([bt[ch], kt[ch]], axis=0))
          for ch in chunks]
    n_ab = [jnp.where(strict, sc[ch][:c2, :c2], 0.0) for ch in chunks]
    a_ak = [jnp.where(strict, sc[ch][:c2, c2:], 0.0) for ch in chunks]
    a_rb = [jnp.where(incl, sc[ch][c2:, :c2], 0.0) for ch in chunks]
    a_rk = [jnp.where(incl, sc[ch][c2:, c2:], 0.0) for ch in chunks]

    p = [eye + n_ab[ch] for ch in chunks]
    nk = [_mm(n_ab[ch], n_ab[ch]) for ch in chunks]
    akv = [_mm(a_ak[ch], ve[ch]) for ch in chunks]
    power = 2
    while power < c:
        if 2 * power < c:
            both = [_mm(nk[ch], jnp.concatenate([nk[ch], p[ch]], axis=1)) for ch in chunks]
            nk = [both[ch][:, :c2] for ch in chunks]
            p = [p[ch] + both[ch][:, c2:] for ch in chunks]
        else:
            p = [p[ch] + _mm(nk[ch], p[ch]) for ch in chunks]
        power *= 2

    ta = [_mm(p[ch], jnp.concatenate([at[ch], akv[ch]], axis=1)) for ch in chunks]
    zero = jnp.zeros((c2, LANES), F32)
    rhs = [jnp.concatenate([ta[ch], jnp.concatenate([zero, ve[ch]], axis=1)], axis=0) for ch in chunks]
    lhs = [jnp.concatenate([jnp.concatenate([a_rb[ch], a_rk[ch]], axis=1),
                            jnp.concatenate([bb[ch].T, kb[ch].T], axis=1)], axis=0) for ch in chunks]
    out = [_mm(lhs[ch], rhs[ch]) for ch in chunks]
    r_hat = [rt[ch] + out[ch][:c2, :LANES] for ch in chunks]
    y_in = [out[ch][:c2, LANES:] for ch in chunks]
    m_state = [eye * jnp.exp(cum_end[ch]) + out[ch][c2:, :LANES] for ch in chunks]
    g_state = [out[ch][c2:, LANES:] for ch in chunks]

    h = [h_ref[p] for p in range(n_pairs)]
    ys = [[] for _ in range(n_pairs)]
    for u in chunks:
        p = units[u][0]
        both = _mm(jnp.concatenate([r_hat[u], m_state[u]], axis=0), h[p])
        ye = both[:c2] + y_in[u]
        h[p] = both[c2:] + g_state[u]
        ys[p].append(ye[:c] + ye[c:])
    for p in range(n_pairs):
        h_ref[p] = h[p]
    y = jnp.concatenate([jnp.concatenate(ys[p], axis=0) for p in range(n_pairs)], axis=1)

    mean = _sum_heads(y, head_ones) * (1.0 / HEAD)
    d = y - mean
    var = _sum_heads(d * d, head_ones) * (1.0 / HEAD)
    yn = d * lax.rsqrt(var + GN_EPS) * lng_ref[...] + lnb_ref[...]
    o_ref[0] = ((yn + bonus) * gate).astype(o_ref.dtype)


def _rwkv_branch(z3, mu_r, mu_k, mu_v, mu_l, w0, wup, a0, aup, gup, k_k, k_a, r_k, lnx_g, lnx_b,
                 *, a_width, rkv_block, lora_block, windows, ts, out_dtype):
    b, s, _ = z3.shape
    n_pairs = a_width // LANES
    lw = mu_l.shape[1]

    def col(blk):
        return lambda bi, si: (bi, si, blk)

    def full(rows, width):
        return pl.BlockSpec((rows, width), lambda bi, si: (0, 0))

    in_specs = [
        pl.BlockSpec((1, ts, a_width), col(rkv_block)),
        pl.BlockSpec((1, ts, a_width), col(rkv_block + 1)),
        pl.BlockSpec((1, ts, a_width), col(rkv_block + 2)),
        pl.BlockSpec((1, ts, lw), col(lora_block)),
        full(1, a_width), full(1, a_width), full(1, a_width), full(1, lw),
        full(1, a_width), full(wup.shape[0], a_width), full(1, a_width), full(aup.shape[0], a_width),
        full(gup.shape[0], a_width),
        full(1, a_width), full(1, a_width), full(1, a_width), full(1, a_width), full(1, a_width),
    ]
    return pl.pallas_call(
        functools.partial(_rwkv_kernel, n_chunks=ts // CHUNK, n_pairs=n_pairs, windows=windows),
        grid=(b, s // ts),
        in_specs=in_specs,
        out_specs=pl.BlockSpec((1, ts, a_width), lambda bi, si: (bi, si, 0)),
        out_shape=jax.ShapeDtypeStruct((b, s, a_width), out_dtype),
        scratch_shapes=[pltpu.VMEM((1, a_width), F32), pltpu.VMEM((1, a_width), F32),
                        pltpu.VMEM((1, a_width), F32), pltpu.VMEM((1, lw), F32),
                        pltpu.VMEM((n_pairs, LANES, LANES), F32)],
        compiler_params=_params(("parallel", "arbitrary")),
        name="rwkv7_chunked",
    )(z3, z3, z3, z3, mu_r, mu_k, mu_v, mu_l, w0, wup, a0, aup, gup, k_k, k_a, r_k, lnx_g, lnx_b)


def _attn_kernel(q_ref, k_ref, v_ref, bias_ref, lq1_ref, lk1_ref, lq2_ref, lk2_ref, sg_ref, o_ref,
                 s_ref, p_ref, va_ref, *, n_qblocks, lambda_init):
    qr = Q_BLOCK
    lam = (jnp.exp(jnp.sum(lq1_ref[...] * lk1_ref[...])) - jnp.exp(jnp.sum(lq2_ref[...] * lk2_ref[...]))
           + lambda_init)
    lane_lo = lax.broadcasted_iota(jnp.int32, (qr, LANES), 1) < HEAD
    scale = HEAD ** -0.5 * LOG2E

    def key_tiles(qb):
        return [(j, min(2, qb + 1 - j)) for j in range(0, qb + 1, 2)]

    def logits(qb):
        rows = slice(qb * qr, (qb + 1) * qr)
        va_ref[rows, :LANES] = v_ref[0, rows, :].astype(BF16)
        va_ref[rows, LANES:] = jnp.ones((qr, LANES), BF16)
        q = (q_ref[0, rows, :].astype(F32) * scale).astype(BF16)
        qe = _expand(q, lane_lo)
        m_l = None
        for j0, nb in key_tiles(qb):
            s = _mm_nt(qe, k_ref[0, j0 * qr:(j0 + nb) * qr, :])
            for t in range(nb):
                j = j0 + t
                sj = s[:, t * qr:(t + 1) * qr]
                if qb - j < 2:
                    sj = sj + bias_ref[0, qb - j]
                s_ref[qb % 2, :, j * qr:(j + 1) * qr] = sj
                m_l = sj if m_l is None else jnp.maximum(m_l, sj)
        return m_l

    def weighted_values(qb, m_l):
        m = jnp.max(m_l, axis=-1, keepdims=True)
        for j in range(qb + 1):
            pj = jnp.exp2(s_ref[qb % 2, :, j * qr:(j + 1) * qr] - m)
            p_ref[qb % 2, :, j * qr:(j + 1) * qr] = pj.astype(BF16)
        n_keys = (qb + 1) * qr
        acc = jnp.dot(p_ref[qb % 2, :, :n_keys], va_ref[:n_keys, :], preferred_element_type=F32)
        o = acc[:, :LANES] / acc[:, LANES:]
        o = o[:qr] - lam * o[qr:]
        o = o * lax.rsqrt(jnp.mean(o * o, axis=-1, keepdims=True) + SUBLN_EPS) * sg_ref[...]
        o = o * (1.0 - lambda_init)
        o_ref[0, qb * qr:(qb + 1) * qr, :] = o.astype(o_ref.dtype)

    pending = logits(0)
    for qb in range(n_qblocks):
        nxt = logits(qb + 1) if qb + 1 < n_qblocks else None
        weighted_values(qb, pending)
        pending = nxt


def _attn_branch(z3, bias, lq1, lk1, lq2, lk2, subln_g, *, q_block, n_heads, lambda_init, out_dtype):
    b, s, _ = z3.shape

    def col(off):
        return lambda bi, hh: (bi, 0, off + hh)

    def small(n):
        return pl.BlockSpec((1, n), lambda bi, hh: (0, 0))

    return pl.pallas_call(
        functools.partial(_attn_kernel, n_qblocks=s // Q_BLOCK, lambda_init=lambda_init),
        grid=(b, n_heads),
        in_specs=[pl.BlockSpec((1, s, LANES), col(q_block)),
                  pl.BlockSpec((1, s, LANES), col(q_block + n_heads)),
                  pl.BlockSpec((1, s, LANES), col(q_block + 2 * n_heads)),
                  pl.BlockSpec((1, 2, 2 * Q_BLOCK, Q_BLOCK), lambda bi, hh: (hh, 0, 0, 0)),
                  small(HEAD), small(HEAD), small(HEAD), small(HEAD), small(LANES)],
        out_specs=pl.BlockSpec((1, s, LANES), lambda bi, hh: (bi, 0, hh)),
        out_shape=jax.ShapeDtypeStruct((b, s, n_heads * LANES), out_dtype),
        scratch_shapes=[pltpu.VMEM((2, 2 * Q_BLOCK, s), F32), pltpu.VMEM((2, 2 * Q_BLOCK, s), BF16),
                        pltpu.VMEM((s, 2 * LANES), BF16)],
        compiler_params=_params(("parallel", "parallel")),
        name="diff_attention",
    )(z3, z3, z3, bias, lq1, lk1, lq2, lk2, subln_g)


def _t5_bucket(rel):
    nb = N_BUCKETS // 2
    max_exact = nb // 2
    ret = jnp.where(rel > 0, nb, 0)
    n = jnp.abs(rel)
    nf = jnp.maximum(n, 1).astype(jnp.float32)
    large = max_exact + (jnp.log(nf / max_exact) / math.log(MAX_DISTANCE / max_exact)
                         * (nb - max_exact)).astype(jnp.int32)
    large = jnp.minimum(large, nb - 1)
    return ret + jnp.where(n < max_exact, n, large)


def _bias_tiles(rel_bias, n_heads):
    assert Q_BLOCK >= MAX_DISTANCE
    dist = jnp.arange(3)[:, None, None]
    qi = jnp.arange(Q_BLOCK)[None, :, None]
    kj = jnp.arange(Q_BLOCK)[None, None, :]
    bucket = _t5_bucket(kj - qi - dist * Q_BLOCK)
    onehot = (bucket[..., None] == jnp.arange(N_BUCKETS)).astype(F32)
    t = jnp.einsum("dqkn,nh->hdqk", onehot, rel_bias.astype(F32), precision=lax.Precision.HIGHEST)
    allowed = (dist > 0) | ((kj // CHUNK) <= (qi // CHUNK))
    t = jnp.where(allowed[None], t, NEG_INF)
    t = (t[:, :2] - t[:, 2:]) * LOG2E
    t = t.reshape(n_heads, 2, 2, Q_BLOCK, Q_BLOCK).transpose(0, 2, 1, 3, 4)
    return t.reshape(n_heads, 2, 2 * Q_BLOCK, Q_BLOCK)


def _merge_out_kernel(ya_ref, ob_ref, ga_ref, gb_ref, x_ref, pa_ref, pb_ref, wo_ref, o_ref, mg_ref):
    half = mg_ref.shape[1] // 2
    for c0 in (0, half):
        cols = slice(c0, c0 + half)
        oa = jnp.dot(ya_ref[...], pa_ref[:, cols], preferred_element_type=F32)
        ob = jnp.dot(ob_ref[...], pb_ref[:, cols], preferred_element_type=F32)
        mg_ref[:, cols] = (jax.nn.sigmoid(ga_ref[:, cols].astype(F32)) * oa
                           + jax.nn.sigmoid(gb_ref[:, cols].astype(F32)) * ob).astype(BF16)
    o_ref[...] = x_ref[...] + jnp.dot(mg_ref[...], wo_ref[...], preferred_element_type=F32)


def _merge_out(ya, ob, z2, x, p_a, p_b, w_out, *, gate_block, tm):
    m, ka = ya.shape
    kb = ob.shape[1]
    d = w_out.shape[1]

    def resident(shape):
        return pl.BlockSpec(shape, lambda i: (0, 0), pipeline_mode=pl.Buffered(1))

    return pl.pallas_call(
        _merge_out_kernel,
        grid=(m // tm,),
        in_specs=[pl.BlockSpec((tm, ka), lambda i: (i, 0)),
                  pl.BlockSpec((tm, kb), lambda i: (i, 0)),
                  pl.BlockSpec((tm, d), lambda i: (i, gate_block)),
                  pl.BlockSpec((tm, d), lambda i: (i, gate_block + 1)),
                  pl.BlockSpec((tm, d), lambda i: (i, 0)),
                  resident(p_a.shape), resident(p_b.shape), resident(w_out.shape)],
        out_specs=pl.BlockSpec((tm, d), lambda i: (i, 0)),
        out_shape=jax.ShapeDtypeStruct((m, d), F32),
        scratch_shapes=[pltpu.VMEM((tm, d), BF16)],
        compiler_params=_params(("parallel",)),
        name="merge_out_proj",
    )(ya, ob, z2, z2, x, p_a, p_b, w_out)


def _ffn_kernel(h_ref, g1_ref, w1_ref, w2_ref, g2_ref, o_ref, m_ref):
    j = pl.program_id(1)

    @pl.when(j == 0)
    def _():
        x = h_ref[...]
        ms = jnp.mean(x * x, axis=-1, keepdims=True)
        m_ref[...] = (x * lax.rsqrt(ms + RMS_EPS) * g1_ref[...]).astype(BF16)
        o_ref[...] = x

    f = jnp.dot(m_ref[...], w1_ref[...], preferred_element_type=F32)
    f = jnp.square(jnp.maximum(f, 0.0)).astype(BF16)
    o_ref[...] += jnp.dot(f, w2_ref[...], preferred_element_type=F32)

    @pl.when(j == pl.num_programs(1) - 1)
    def _():
        y = o_ref[...]
        ms = jnp.mean(y * y, axis=-1, keepdims=True)
        o_ref[...] = y * lax.rsqrt(ms + RMS_EPS) * g2_ref[...]


def _ffn(h, g1, w1, w2, g2, *, tm, tf):
    m, d = h.shape
    dff = w1.shape[1]
    return pl.pallas_call(
        _ffn_kernel,
        grid=(m // tm, dff // tf),
        in_specs=[pl.BlockSpec((tm, d), lambda i, j: (i, 0)),
                  pl.BlockSpec((1, d), lambda i, j: (0, 0)),
                  pl.BlockSpec((d, tf), lambda i, j: (0, j)),
                  pl.BlockSpec((tf, d), lambda i, j: (j, 0)),
                  pl.BlockSpec((1, d), lambda i, j: (0, 0))],
        out_specs=pl.BlockSpec((tm, d), lambda i, j: (i, 0)),
        out_shape=jax.ShapeDtypeStruct((m, d), F32),
        scratch_shapes=[pltpu.VMEM((tm, d), BF16)],
        compiler_params=_params(("parallel", "arbitrary")),
        name="ffn_residual_norm",
    )(h, g1, w1, w2, g2)


def _pad_cols(w, width):
    return jnp.pad(w, ((0, 0), (0, width - w.shape[1])))


def _regroup_kernel(wt_ref, o_ref, *, pieces):
    col = 0
    for src, width in pieces:
        for c0 in range(0, width, LANES):
            cw = min(LANES, width - c0)
            if src is None:
                o_ref[:, col + c0:col + c0 + cw] = jnp.zeros((o_ref.shape[0], cw), o_ref.dtype)
            else:
                assert src + c0 + LANES <= wt_ref.shape[1]
                blk = wt_ref[0, src + c0:src + c0 + LANES, :].T
                o_ref[:, col + c0:col + c0 + cw] = blk[:, :cw].astype(o_ref.dtype)
        col += width


def _regroup_weight(wt, layer, pieces, *, tk):
    _, n, k = wt.shape
    out_cols = sum(width for _, width in pieces)
    return pl.pallas_call(
        functools.partial(_regroup_kernel, pieces=pieces),
        grid=(k // tk,),
        in_specs=[pl.BlockSpec((1, n, tk), lambda i: (layer, 0, i))],
        out_specs=pl.BlockSpec((tk, out_cols), lambda i: (i, 0)),
        out_shape=jax.ShapeDtypeStruct((k, out_cols), BF16),
        compiler_params=_params(("parallel",)),
        name="regroup_weight",
    )(wt)


def _tile_plan(m, s, z_cols, d_ff):
    plan = dict(
        inproj_tm=min(1024, m), inproj_tn=1536,
        rwkv_ts=4 * CHUNK,
        merge_tm=min(512, m),
        ffn_tm=min(512, m), ffn_tf=1024)
    assert m % plan["inproj_tm"] == 0 and z_cols % plan["inproj_tn"] == 0
    assert s % plan["rwkv_ts"] == 0 and s % (2 * Q_BLOCK) == 0
    assert m % plan["merge_tm"] == 0 and m % plan["ffn_tm"] == 0 and d_ff % plan["ffn_tf"] == 0
    return plan


def kernel(x, norm_mix_g, w_in, mu_shift, w0, w_up, a0, a_up, g_up, k_k, k_a, r_k, lnx_g, lnx_b, lambda_q1, lambda_k1, lambda_q2, lambda_k2, subln_g, rel_bias, p_a, p_b, w_out, norm_mlp_g, w_ff1, w_ff2, norm_final_g):
    b, s, d = x.shape
    depth = w_in.shape[0]
    a_width = w0.shape[1]
    d_lora, a_lora, g_lora = w_up.shape[1], a_up.shape[1], g_up.shape[1]
    n_bheads = rel_bias.shape[1] // 2
    b_width = n_bheads * LANES
    assert depth == 1, "the final norm is fused into the (single) layer's MLP kernel"
    assert a_width % MXU_DIM == 0 and subln_g.shape[1] == LANES
    m = b * s

    o_wd = 3 * a_width
    o_b = o_wd + d_lora + a_lora + g_lora
    o_g = o_b + 3 * b_width
    lora_w = -(-(o_b - o_wd) // (4 * LANES)) * (4 * LANES)
    rkv_off = 2 * d
    lora_off = rkv_off + 3 * a_width
    bq_off = lora_off + lora_w
    assert lora_off % lora_w == 0

    def window(off, width):
        return (off // LANES * LANES, -(-(off + width) // LANES) * LANES)

    def embed_rows(w, off, win):
        return jnp.pad(w, ((off - win[0], win[1] - off - w.shape[0]), (0, 0))).astype(BF16)

    windows = (window(0, d_lora), window(d_lora, a_lora), window(d_lora + a_lora, g_lora))

    h = x.reshape(m, d)
    l = 0
    lambda_init = 0.8 - 0.6 * math.exp(-0.3 * l)
    w_in_r = _regroup_weight(
        jnp.swapaxes(w_in, 1, 2), l,
        [(o_g, w_in.shape[2] - o_g), (0, o_b), (None, lora_w - (o_b - o_wd)), (o_b, o_g - o_b)],
        tk=min(256, d))
    mu = mu_shift[l][None, :]
    mu_l = _pad_cols(mu[:, o_wd:o_b], lora_w)
    plan = _tile_plan(m, s, w_in_r.shape[1], w_ff1.shape[2])

    z2 = _norm_matmul(h, norm_mix_g[l][None, :], w_in_r, tm=plan["inproj_tm"], tn=plan["inproj_tn"],
                      relu2=False, out_dtype=BF16)
    z3 = z2.reshape(b, s, z2.shape[1])

    ya = _rwkv_branch(
        z3, mu[:, :a_width], mu[:, a_width:2 * a_width], mu[:, 2 * a_width:3 * a_width], mu_l,
        w0[l][None, :], embed_rows(w_up[l], 0, windows[0]), a0[l][None, :],
        embed_rows(a_up[l], d_lora, windows[1]), embed_rows(g_up[l], d_lora + a_lora, windows[2]),
        k_k[l][None, :], k_a[l][None, :], r_k[l].reshape(1, a_width), lnx_g[l][None, :], lnx_b[l][None, :],
        a_width=a_width, rkv_block=rkv_off // a_width, lora_block=lora_off // lora_w, windows=windows,
        ts=plan["rwkv_ts"], out_dtype=BF16)

    ob = _attn_branch(
        z3, _bias_tiles(rel_bias, n_bheads), lambda_q1[l][None, :], lambda_k1[l][None, :],
        lambda_q2[l][None, :], lambda_k2[l][None, :], subln_g[l][None, :],
        q_block=bq_off // LANES, n_heads=n_bheads, lambda_init=lambda_init, out_dtype=BF16)

    h = _merge_out(ya.reshape(m, a_width), ob.reshape(m, b_width), z2, h, p_a[l].astype(BF16),
                   p_b[l].astype(BF16), w_out[l].astype(BF16), gate_block=0, tm=plan["merge_tm"])
    h = _ffn(h, norm_mlp_g[l][None, :], w_ff1[l].astype(BF16), w_ff2[l].astype(BF16),
             norm_final_g[None, :], tm=plan["ffn_tm"], tf=plan["ffn_tf"])
    return h.reshape(b, s, d)
```

```python
import functools
import math

import jax
import jax.numpy as jnp
from jax import lax
from jax.experimental import pallas as pl
from jax.experimental.pallas import tpu as pltpu

F32 = jnp.float32
BF16 = jnp.bfloat16

LANES = 128
HEAD = 64
CHUNK = 64
Q_BLOCK = 128
MXU_DIM = 256
BF16_ROWS = 16
KEY_TILE_BLOCKS = 2
RMS_EPS = 1e-6
GN_EPS = 64e-5
SUBLN_EPS = 1e-5
N_BUCKETS = 32
MAX_DISTANCE = 128
NEG_INF = -1e30
LOG2E = 1.4426950408889634
VMEM_LIMIT = 56 * 1024 * 1024


def _mm(a, b):
    return jnp.dot(a.astype(BF16), b.astype(BF16), preferred_element_type=F32)


def _mm_nt(a, b):
    return lax.dot_general(a.astype(BF16), b.astype(BF16), (((1,), (1,)), ((), ())),
                           preferred_element_type=F32)


def _params(sem):
    return pltpu.CompilerParams(dimension_semantics=sem, vmem_limit_bytes=VMEM_LIMIT)


def _norm_matmul_kernel(x_ref, g_ref, w_ref, o_ref, u0_ref, u1_ref, *, n_chunks):
    i = pl.program_id(0)
    rc = x_ref.shape[0] // n_chunks

    def norm_chunk(dst_ref):
        r0 = pl.multiple_of(jnp.minimum(pl.program_id(1), n_chunks - 1) * rc, rc)
        x = x_ref[pl.ds(r0, rc), :]
        ms = jnp.mean(x * x, axis=-1, keepdims=True)
        dst_ref[pl.ds(r0, rc), :] = (x * lax.rsqrt(ms + RMS_EPS) * g_ref[...]).astype(BF16)

    def step(dst_ref, src_ref):
        norm_chunk(dst_ref)
        o_ref[...] = jnp.dot(src_ref[...], w_ref[...], preferred_element_type=F32).astype(o_ref.dtype)

    pl.when(i == 0)(lambda: norm_chunk(u0_ref))
    pl.when((i > 0) & (i % 2 == 1))(lambda: step(u1_ref, u0_ref))
    pl.when((i > 0) & (i % 2 == 0))(lambda: step(u0_ref, u1_ref))


def _norm_matmul(x, g, w, *, tm, tn, n_chunks, out_dtype):
    m, d = x.shape
    n = w.shape[1]
    n_i = m // tm
    assert n // tn >= n_chunks and tm % n_chunks == 0
    return pl.pallas_call(
        functools.partial(_norm_matmul_kernel, n_chunks=n_chunks),
        grid=(n_i + 1, n // tn),
        in_specs=[pl.BlockSpec((tm, d), lambda i, j: (jnp.minimum(i, n_i - 1), 0)),
                  pl.BlockSpec((1, d), lambda i, j: (0, 0)),
                  pl.BlockSpec((d, tn), lambda i, j: (0, j))],
        out_specs=pl.BlockSpec((tm, tn), lambda i, j: (jnp.maximum(i - 1, 0), jnp.where(i == 0, 0, j))),
        out_shape=jax.ShapeDtypeStruct((m, n), out_dtype),
        scratch_shapes=[pltpu.VMEM((tm, d), BF16), pltpu.VMEM((tm, d), BF16)],
        compiler_params=_params(("arbitrary", "arbitrary")),
        name="norm_matmul",
    )(x, g, w)


def _expand(t, lane_lo):
    zero = jnp.zeros_like(t)
    return jnp.concatenate([jnp.where(lane_lo, t, zero), jnp.where(lane_lo, zero, t)], axis=0)


def _sum_heads(x, head_ones):
    xb = x.astype(BF16)
    g = head_ones.shape[0]
    return jnp.concatenate(
        [jnp.dot(xb[:, i * g:(i + 1) * g], head_ones, preferred_element_type=F32)
         for i in range(x.shape[1] // g)], axis=1)


def _chunk_cumsum(x, tri):
    hi = x.astype(BF16)
    lo = (x - hi.astype(F32)).astype(BF16)
    out = jnp.dot(tri, jnp.concatenate([hi, lo], axis=1), preferred_element_type=F32)
    return out[:, :x.shape[1]] + out[:, x.shape[1]:]


def _rwkv_kernel(zr_ref, zk_ref, zv_ref, zl_ref, mur_ref, muk_ref, muv_ref, mul_ref,
                 w0_ref, wup_ref, a0_ref, aup_ref, gup_ref, kk_ref, ka_ref, rk_ref, lng_ref, lnb_ref,
                 o_ref, cr_ref, ck_ref, cv_ref, cl_ref, h_ref, *, n_chunks, n_pairs, windows):
    c = CHUNK
    c2 = 2 * c
    ts = n_chunks * c
    units = [(p, ch) for ch in range(n_chunks) for p in range(n_pairs)]
    chunks = range(len(units))

    @pl.when(pl.program_id(1) == 0)
    def _():
        cr_ref[...] = jnp.zeros_like(cr_ref)
        ck_ref[...] = jnp.zeros_like(ck_ref)
        cv_ref[...] = jnp.zeros_like(cv_ref)
        cl_ref[...] = jnp.zeros_like(cl_ref)
        h_ref[...] = jnp.zeros_like(h_ref)

    def shifted(z, carry_ref, mu):
        rows = lax.broadcasted_iota(jnp.int32, z.shape, 0)
        prev = jnp.where(rows == 0, carry_ref[...], pltpu.roll(z, 1, 0))
        carry_ref[...] = z[z.shape[0] - 1:, :]
        return z + (prev - z) * mu

    ri = lax.broadcasted_iota(jnp.int32, (c2, c2), 0)
    ci = lax.broadcasted_iota(jnp.int32, (c2, c2), 1)
    strict = ci < ri
    incl = ci <= ri
    eye = (ci == ri).astype(F32)
    hr = lax.broadcasted_iota(jnp.int32, (MXU_DIM, MXU_DIM), 0)
    hc = lax.broadcasted_iota(jnp.int32, (MXU_DIM, MXU_DIM), 1)
    head_ones = ((hr // HEAD) == (hc // HEAD)).astype(BF16)
    ti = lax.broadcasted_iota(jnp.int32, (ts, ts), 0)
    tj = lax.broadcasted_iota(jnp.int32, (ts, ts), 1)
    tri = ((tj <= ti) & ((tj // c) == (ti // c))).astype(BF16)
    lane_lo = lax.broadcasted_iota(jnp.int32, (c, LANES), 1) < HEAD

    r = shifted(zr_ref[0].astype(F32), cr_ref, mur_ref[...])
    k = shifted(zk_ref[0].astype(F32), ck_ref, muk_ref[...])
    v = shifted(zv_ref[0].astype(F32), cv_ref, muv_ref[...])
    lo = shifted(zl_ref[0].astype(F32), cl_ref, mul_ref[...])

    (d0, d1), (a0_, a1_), (g0, g1) = windows
    t_w = w0_ref[...] + _mm(jnp.tanh(lo[:, d0:d1]), wup_ref[...])
    lw = -math.exp(-0.5) * jax.nn.sigmoid(t_w)
    a_sig = jax.nn.sigmoid(a0_ref[...] + _mm(lo[:, a0_:a1_], aup_ref[...]))
    gate = _mm(jax.nn.sigmoid(lo[:, g0:g1]), gup_ref[...])
    kk = k * kk_ref[...]
    kk = kk * lax.rsqrt(jnp.maximum(_sum_heads(kk * kk, head_ones), 1e-24))
    k2 = k * (1.0 + (a_sig - 1.0) * ka_ref[...])
    b_v = kk * a_sig
    bonus = _sum_heads(r * k2 * rk_ref[...], head_ones) * v
    cum = _chunk_cumsum(lw, tri)
    g_in = jnp.exp(cum)
    g_inv = jnp.exp(-cum)
    rt_all = r * g_in
    at_all = -kk * jnp.exp(cum - lw)
    kt_all = k2 * g_inv
    bt_all = b_v * g_inv

    def tile(x, u):
        p, ch = units[u]
        return x[ch * c:(ch + 1) * c, p * LANES:(p + 1) * LANES]

    cum_end = [tile(cum, u)[c - 1:, :] for u in chunks]
    g_tail = [jnp.exp(cum_end[u] - tile(cum, u)) for u in chunks]
    rt = [_expand(tile(rt_all, u), lane_lo) for u in chunks]
    at = [_expand(tile(at_all, u), lane_lo) for u in chunks]
    kt = [_expand(tile(kt_all, u), lane_lo) for u in chunks]
    bt = [_expand(tile(bt_all, u), lane_lo) for u in chunks]
    kb = [_expand(tile(k2, u) * g_tail[u], lane_lo) for u in chunks]
    bb = [_expand(tile(b_v, u) * g_tail[u], lane_lo) for u in chunks]
    ve = [_expand(tile(v, u), lane_lo) for u in chunks]

    sc = [_mm_nt(jnp.concatenate([at[ch], rt[ch]], axis=0), jnp.concatenate([bt[ch], kt[ch]], axis=0))
          for ch in chunks]
    n_ab = [jnp.where(strict, sc[ch][:c2, :c2], 0.0) for ch in chunks]
    a_ak = [jnp.where(strict, sc[ch][:c2, c2:], 0.0) for ch in chunks]
    a_rb = [jnp.where(incl, sc[ch][c2:, :c2], 0.0) for ch in chunks]
    a_rk = [jnp.where(incl, sc[ch][c2:, c2:], 0.0) for ch in chunks]

    p = [eye + n_ab[ch] for ch in chunks]
    nk = [_mm(n_ab[ch], n_ab[ch]) for ch in chunks]
    akv = [_mm(a_ak[ch], ve[ch]) for ch in chunks]
    power = 2
    while power < c:
        if 2 * power < c:
            both = [_mm(nk[ch], jnp.concatenate([nk[ch], p[ch]], axis=1)) for ch in chunks]
            nk = [both[ch][:, :c2] for ch in chunks]
            p = [p[ch] + both[ch][:, c2:] for ch in chunks]
        else:
            p = [p[ch] + _mm(nk[ch], p[ch]) for ch in chunks]
        power *= 2

    ta = [_mm(p[ch], jnp.concatenate([at[ch], akv[ch]], axis=1)) for ch in chunks]
    zero = jnp.zeros((c2, LANES), F32)
    rhs = [jnp.concatenate([ta[ch], jnp.concatenate([zero, ve[ch]], axis=1)], axis=0) for ch in chunks]
    lhs = [jnp.concatenate([jnp.concatenate([a_rb[ch], a_rk[ch]], axis=1),
                            jnp.concatenate([bb[ch].T, kb[ch].T], axis=1)], axis=0) for ch in chunks]
    out = [_mm(lhs[ch], rhs[ch]) for ch in chunks]
    r_hat = [rt[ch] + out[ch][:c2, :LANES] for ch in chunks]
    y_in = [out[ch][:c2, LANES:] for ch in chunks]
    m_state = [eye * jnp.exp(cum_end[ch]) + out[ch][c2:, :LANES] for ch in chunks]
    g_state = [out[ch][c2:, LANES:] for ch in chunks]

    h = [h_ref[p] for p in range(n_pairs)]
    ys = [[] for _ in range(n_pairs)]
    for u in chunks:
        p = units[u][0]
        both = _mm(jnp.concatenate([r_hat[u], m_state[u]], axis=0), h[p])
        ye = both[:c2] + y_in[u]
        h[p] = both[c2:] + g_state[u]
        ys[p].append(ye[:c] + ye[c:])
    for p in range(n_pairs):
        h_ref[p] = h[p]
    y = jnp.concatenate([jnp.concatenate(ys[p], axis=0) for p in range(n_pairs)], axis=1)

    mean = _sum_heads(y, head_ones) * (1.0 / HEAD)
    d = y - mean
    var = _sum_heads(d * d, head_ones) * (1.0 / HEAD)
    yn = d * lax.rsqrt(var + GN_EPS) * lng_ref[...] + lnb_ref[...]
    o_ref[0] = ((yn + bonus) * gate).astype(o_ref.dtype)


def _rwkv_branch(z3, mu_r, mu_k, mu_v, mu_l, w0, wup, a0, aup, gup, k_k, k_a, r_k, lnx_g, lnx_b,
                 *, a_width, rkv_block, lora_block, windows, ts, out_dtype):
    b, s, _ = z3.shape
    n_pairs = a_width // LANES
    lw = mu_l.shape[1]

    def col(blk):
        return lambda bi, si: (bi, si, blk)

    def full(rows, width):
        return pl.BlockSpec((rows, width), lambda bi, si: (0, 0))

    in_specs = [
        pl.BlockSpec((1, ts, a_width), col(rkv_block)),
        pl.BlockSpec((1, ts, a_width), col(rkv_block + 1)),
        pl.BlockSpec((1, ts, a_width), col(rkv_block + 2)),
        pl.BlockSpec((1, ts, lw), col(lora_block)),
        full(1, a_width), full(1, a_width), full(1, a_width), full(1, lw),
        full(1, a_width), full(wup.shape[0], a_width), full(1, a_width), full(aup.shape[0], a_width),
        full(gup.shape[0], a_width),
        full(1, a_width), full(1, a_width), full(1, a_width), full(1, a_width), full(1, a_width),
    ]
    return pl.pallas_call(
        functools.partial(_rwkv_kernel, n_chunks=ts // CHUNK, n_pairs=n_pairs, windows=windows),
        grid=(b, s // ts),
        in_specs=in_specs,
        out_specs=pl.BlockSpec((1, ts, a_width), lambda bi, si: (bi, si, 0)),
        out_shape=jax.ShapeDtypeStruct((b, s, a_width), out_dtype),
        scratch_shapes=[pltpu.VMEM((1, a_width), F32), pltpu.VMEM((1, a_width), F32),
                        pltpu.VMEM((1, a_width), F32), pltpu.VMEM((1, lw), F32),
                        pltpu.VMEM((n_pairs, LANES, LANES), F32)],
        compiler_params=_params(("parallel", "arbitrary")),
        name="rwkv7_chunked",
    )(z3, z3, z3, z3, mu_r, mu_k, mu_v, mu_l, w0, wup, a0, aup, gup, k_k, k_a, r_k, lnx_g, lnx_b)


def _attn_kernel(q_ref, k_ref, v_ref, bias_ref, lq1_ref, lk1_ref, lq2_ref, lk2_ref, sg_ref, o_ref,
                 s_ref, p_ref, vt_ref, *, n_qblocks, lambda_init):
    qr = Q_BLOCK
    lam = (jnp.exp(jnp.sum(lq1_ref[...] * lk1_ref[...])) - jnp.exp(jnp.sum(lq2_ref[...] * lk2_ref[...]))
           + lambda_init)
    lane_lo = lax.broadcasted_iota(jnp.int32, (qr, LANES), 1) < HEAD
    scale = HEAD ** -0.5 * LOG2E
    n_ones = vt_ref.shape[0] - LANES
    n_buf = s_ref.shape[0]

    def key_tiles(qb):
        return [(j, min(KEY_TILE_BLOCKS, qb + 1 - j)) for j in range(0, qb + 1, KEY_TILE_BLOCKS)]

    def logits(qb):
        rows = slice(qb * qr, (qb + 1) * qr)
        vt_ref[:LANES, rows] = v_ref[0, rows, :].astype(F32).T.astype(BF16)
        vt_ref[LANES:, rows] = jnp.ones((n_ones, qr), BF16)
        q = q_ref[0, rows, :].astype(F32) * scale
        qe_t = _expand(q, lane_lo).T.astype(BF16)
        m = None
        for j0, nb in key_tiles(qb):
            s = jnp.dot(k_ref[0, j0 * qr:(j0 + nb) * qr, :], qe_t, preferred_element_type=F32)
            for t in range(nb):
                j = j0 + t
                sj = s[t * qr:(t + 1) * qr, :]
                if qb - j < 2:
                    sj = sj + bias_ref[0, qb - j]
                s_ref[qb % n_buf, j * qr:(j + 1) * qr, :] = sj
                m_j = jnp.max(sj, axis=0, keepdims=True)
                m = m_j if m is None else jnp.maximum(m, m_j)
        return m

    def weighted_values(qb, m):
        for j in range(qb + 1):
            pj = jnp.exp2(s_ref[qb % n_buf, j * qr:(j + 1) * qr, :] - m)
            p_ref[qb % n_buf, j * qr:(j + 1) * qr, :] = pj.astype(BF16)
        n_keys = (qb + 1) * qr
        acc = jnp.dot(vt_ref[:, :n_keys], p_ref[qb % n_buf, :n_keys, :], preferred_element_type=F32)
        o = acc[:LANES] / acc[LANES:LANES + 1]
        o = o[:, :qr] - lam * o[:, qr:]
        o = o * lax.rsqrt(jnp.mean(o * o, axis=0, keepdims=True) + SUBLN_EPS) * sg_ref[...]
        o = o * (1.0 - lambda_init)
        o_ref[0, qb * qr:(qb + 1) * qr, :] = o.T.astype(o_ref.dtype)

    ahead = n_buf - 1
    pending = [logits(qb) for qb in range(min(ahead, n_qblocks))]
    for qb in range(n_qblocks):
        if qb + ahead < n_qblocks:
            pending.append(logits(qb + ahead))
        weighted_values(qb, pending.pop(0))


def _attn_branch(z3, bias, lq1, lk1, lq2, lk2, subln_g, *, q_block, n_heads, lambda_init, out_dtype):
    b, s, _ = z3.shape

    def col(off):
        return lambda bi, hh: (bi, 0, off + hh)

    def small(n):
        return pl.BlockSpec((1, n), lambda bi, hh: (0, 0))

    return pl.pallas_call(
        functools.partial(_attn_kernel, n_qblocks=s // Q_BLOCK, lambda_init=lambda_init),
        grid=(b, n_heads),
        in_specs=[pl.BlockSpec((1, s, LANES), col(q_block)),
                  pl.BlockSpec((1, s, LANES), col(q_block + n_heads)),
                  pl.BlockSpec((1, s, LANES), col(q_block + 2 * n_heads)),
                  pl.BlockSpec((1, 2, Q_BLOCK, 2 * Q_BLOCK), lambda bi, hh: (hh, 0, 0, 0)),
                  small(HEAD), small(HEAD), small(HEAD), small(HEAD),
                  pl.BlockSpec((LANES, 1), lambda bi, hh: (0, 0))],
        out_specs=pl.BlockSpec((1, s, LANES), lambda bi, hh: (bi, 0, hh)),
        out_shape=jax.ShapeDtypeStruct((b, s, n_heads * LANES), out_dtype),
        scratch_shapes=[pltpu.VMEM((3, s, 2 * Q_BLOCK), F32), pltpu.VMEM((3, s, 2 * Q_BLOCK), BF16),
                        pltpu.VMEM((LANES + BF16_ROWS, s), BF16)],
        compiler_params=_params(("parallel", "parallel")),
        name="diff_attention",
    )(z3, z3, z3, bias, lq1, lk1, lq2, lk2, subln_g)


def _t5_bucket(rel):
    nb = N_BUCKETS // 2
    max_exact = nb // 2
    ret = jnp.where(rel > 0, nb, 0)
    n = jnp.abs(rel)
    nf = jnp.maximum(n, 1).astype(jnp.float32)
    large = max_exact + (jnp.log(nf / max_exact) / math.log(MAX_DISTANCE / max_exact)
                         * (nb - max_exact)).astype(jnp.int32)
    large = jnp.minimum(large, nb - 1)
    return ret + jnp.where(n < max_exact, n, large)


def _bias_tiles(rel_bias, n_heads):
    assert Q_BLOCK >= MAX_DISTANCE
    dist = jnp.arange(3)[:, None, None]
    qi = jnp.arange(Q_BLOCK)[None, :, None]
    kj = jnp.arange(Q_BLOCK)[None, None, :]
    bucket = _t5_bucket(kj - qi - dist * Q_BLOCK)
    onehot = (bucket[..., None] == jnp.arange(N_BUCKETS)).astype(F32)
    t = jnp.einsum("dqkn,nh->hdqk", onehot, rel_bias.astype(F32), precision=lax.Precision.HIGHEST)
    allowed = (dist > 0) | ((kj // CHUNK) <= (qi // CHUNK))
    t = jnp.where(allowed[None], t, NEG_INF)
    t = (t[:, :2] - t[:, 2:]) * LOG2E
    t = t.reshape(n_heads, 2, 2, Q_BLOCK, Q_BLOCK).transpose(0, 2, 4, 1, 3)
    return t.reshape(n_heads, 2, Q_BLOCK, 2 * Q_BLOCK)


def _merge_out_kernel(ya_ref, ob_ref, ga_ref, gb_ref, x_ref, pa_ref, pb_ref, wo_ref, o_ref, mg_ref):
    half = mg_ref.shape[1] // 2
    for c0 in (0, half):
        cols = slice(c0, c0 + half)
        oa = jnp.dot(ya_ref[...], pa_ref[:, cols], preferred_element_type=F32)
        ob = jnp.dot(ob_ref[...], pb_ref[:, cols], preferred_element_type=F32)
        mg_ref[:, cols] = (jax.nn.sigmoid(ga_ref[:, cols].astype(F32)) * oa
                           + jax.nn.sigmoid(gb_ref[:, cols].astype(F32)) * ob).astype(BF16)
    o_ref[...] = x_ref[...] + jnp.dot(mg_ref[...], wo_ref[...], preferred_element_type=F32)


def _merge_out(ya, ob, z2, x, p_a, p_b, w_out, *, gate_block, tm):
    m, ka = ya.shape
    kb = ob.shape[1]
    d = w_out.shape[1]

    def resident(shape):
        return pl.BlockSpec(shape, lambda i: (0, 0), pipeline_mode=pl.Buffered(1))

    return pl.pallas_call(
        _merge_out_kernel,
        grid=(m // tm,),
        in_specs=[pl.BlockSpec((tm, ka), lambda i: (i, 0)),
                  pl.BlockSpec((tm, kb), lambda i: (i, 0)),
                  pl.BlockSpec((tm, d), lambda i: (i, gate_block)),
                  pl.BlockSpec((tm, d), lambda i: (i, gate_block + 1)),
                  pl.BlockSpec((tm, d), lambda i: (i, 0)),
                  resident(p_a.shape), resident(p_b.shape), resident(w_out.shape)],
        out_specs=pl.BlockSpec((tm, d), lambda i: (i, 0)),
        out_shape=jax.ShapeDtypeStruct((m, d), F32),
        scratch_shapes=[pltpu.VMEM((tm, d), BF16)],
        compiler_params=_params(("parallel",)),
        name="merge_out_proj",
    )(ya, ob, z2, z2, x, p_a, p_b, w_out)


def _ffn_kernel(h_ref, hn_ref, g1_ref, w1_ref, w2_ref, g2_ref, o_ref, m0_ref, m1_ref, *, n_chunks):
    i = pl.program_id(0)
    j = pl.program_id(1)
    rc = hn_ref.shape[0] // n_chunks

    def normalise(x):
        ms = jnp.mean(x * x, axis=-1, keepdims=True)
        return (x * lax.rsqrt(ms + RMS_EPS) * g1_ref[...]).astype(BF16)

    def step(cur_ref, nxt_ref, first):
        r0 = pl.multiple_of(jnp.minimum(j, n_chunks - 1) * rc, rc)
        nxt_ref[pl.ds(r0, rc), :] = normalise(hn_ref[pl.ds(r0, rc), :])
        f = jnp.dot(cur_ref[...], w1_ref[...], preferred_element_type=F32)
        f = jnp.square(jnp.maximum(f, 0.0)).astype(BF16)
        base = h_ref[...] if first else o_ref[...]
        o_ref[...] = base + jnp.dot(f, w2_ref[...], preferred_element_type=F32)

    @pl.when((i == 0) & (j == 0))
    def _():
        m0_ref[...] = normalise(h_ref[...])

    even = i % 2 == 0
    pl.when(even & (j == 0))(lambda: step(m0_ref, m1_ref, True))
    pl.when(even & (j > 0))(lambda: step(m0_ref, m1_ref, False))
    pl.when(jnp.logical_not(even) & (j == 0))(lambda: step(m1_ref, m0_ref, True))
    pl.when(jnp.logical_not(even) & (j > 0))(lambda: step(m1_ref, m0_ref, False))

    @pl.when(j == pl.num_programs(1) - 1)
    def _():
        y = o_ref[...]
        ms = jnp.mean(y * y, axis=-1, keepdims=True)
        o_ref[...] = y * lax.rsqrt(ms + RMS_EPS) * g2_ref[...]


def _ffn(h, g1, w1, w2, g2, *, tm, tf, n_chunks):
    m, d = h.shape
    dff = w1.shape[1]
    n_i = m // tm
    assert dff // tf >= n_chunks and tm % n_chunks == 0
    return pl.pallas_call(
        functools.partial(_ffn_kernel, n_chunks=n_chunks),
        grid=(n_i, dff // tf),
        in_specs=[pl.BlockSpec((tm, d), lambda i, j: (i, 0)),
                  pl.BlockSpec((tm, d), lambda i, j: (jnp.minimum(i + 1, n_i - 1), 0)),
                  pl.BlockSpec((1, d), lambda i, j: (0, 0)),
                  pl.BlockSpec((d, tf), lambda i, j: (0, j)),
                  pl.BlockSpec((tf, d), lambda i, j: (j, 0)),
                  pl.BlockSpec((1, d), lambda i, j: (0, 0))],
        out_specs=pl.BlockSpec((tm, d), lambda i, j: (i, 0)),
        out_shape=jax.ShapeDtypeStruct((m, d), F32),
        scratch_shapes=[pltpu.VMEM((tm, d), BF16), pltpu.VMEM((tm, d), BF16)],
        compiler_params=_params(("arbitrary", "arbitrary")),
        name="ffn_residual_norm",
    )(h, h, g1, w1, w2, g2)


def _pad_cols(w, width):
    return jnp.pad(w, ((0, 0), (0, width - w.shape[1])))


def _regroup_kernel(wt_ref, o_ref, *, pieces):
    col = 0
    for src, width in pieces:
        for c0 in range(0, width, LANES):
            cw = min(LANES, width - c0)
            if src is None:
                o_ref[:, col + c0:col + c0 + cw] = jnp.zeros((o_ref.shape[0], cw), o_ref.dtype)
            else:
                assert src + c0 + LANES <= wt_ref.shape[1]
                blk = wt_ref[0, src + c0:src + c0 + LANES, :].T
                o_ref[:, col + c0:col + c0 + cw] = blk[:, :cw].astype(o_ref.dtype)
        col += width


def _regroup_weight(wt, layer, pieces, *, tk):
    _, n, k = wt.shape
    out_cols = sum(width for _, width in pieces)
    return pl.pallas_call(
        functools.partial(_regroup_kernel, pieces=pieces),
        grid=(k // tk,),
        in_specs=[pl.BlockSpec((1, n, tk), lambda i: (layer, 0, i))],
        out_specs=pl.BlockSpec((tk, out_cols), lambda i: (i, 0)),
        out_shape=jax.ShapeDtypeStruct((k, out_cols), BF16),
        compiler_params=_params(("parallel",)),
        name="regroup_weight",
    )(wt)


def _tile_plan(m, s, z_cols, d_ff):
    plan = dict(
        inproj_tm=min(1024, m), inproj_tn=1536,
        inproj_norm_chunks=4,
        rwkv_ts=4 * CHUNK,
        merge_tm=min(512, m),
        ffn_tm=min(512, m), ffn_tf=1024,
        ffn_norm_chunks=4)
    assert m % plan["inproj_tm"] == 0 and z_cols % plan["inproj_tn"] == 0
    assert s % plan["rwkv_ts"] == 0 and s % (2 * Q_BLOCK) == 0
    assert m % plan["merge_tm"] == 0 and m % plan["ffn_tm"] == 0 and d_ff % plan["ffn_tf"] == 0
    return plan


def kernel(x, norm_mix_g, w_in, mu_shift, w0, w_up, a0, a_up, g_up, k_k, k_a, r_k, lnx_g, lnx_b, lambda_q1, lambda_k1, lambda_q2, lambda_k2, subln_g, rel_bias, p_a, p_b, w_out, norm_mlp_g, w_ff1, w_ff2, norm_final_g):
    b, s, d = x.shape
    depth = w_in.shape[0]
    a_width = w0.shape[1]
    d_lora, a_lora, g_lora = w_up.shape[1], a_up.shape[1], g_up.shape[1]
    n_bheads = rel_bias.shape[1] // 2
    b_width = n_bheads * LANES
    assert depth == 1, "the final norm is fused into the (single) layer's MLP kernel"
    assert a_width % MXU_DIM == 0 and subln_g.shape[1] == LANES
    m = b * s

    o_wd = 3 * a_width
    o_b = o_wd + d_lora + a_lora + g_lora
    o_g = o_b + 3 * b_width
    def lanes_up(n):
        return -(-n // LANES) * LANES

    lora_widths = (d_lora, a_lora, g_lora)
    lora_src = (o_wd, o_wd + d_lora, o_wd + d_lora + a_lora)
    lora_lane = (0, lanes_up(d_lora), lanes_up(d_lora) + lanes_up(a_lora))
    lora_used = lora_lane[2] + lanes_up(g_lora)
    lora_w = -(-lora_used // (4 * LANES)) * (4 * LANES)
    lora_pads = (lanes_up(d_lora) - d_lora, lanes_up(a_lora) - a_lora, lora_w - lora_lane[2] - g_lora)
    rkv_off = 2 * d
    lora_off = rkv_off + 3 * a_width
    bq_off = lora_off + lora_w
    assert lora_off % lora_w == 0

    def embed_rows(w, off, win):
        return jnp.pad(w, ((off - win[0], win[1] - off - w.shape[0]), (0, 0))).astype(BF16)

    windows = tuple((off, off + lanes_up(width)) for off, width in zip(lora_lane, lora_widths))
    lora_pieces = [piece for src, width, pad in zip(lora_src, lora_widths, lora_pads)
                   for piece in ((src, width), (None, pad)) if piece[1]]

    h = x.reshape(m, d)
    l = 0
    lambda_init = 0.8 - 0.6 * math.exp(-0.3 * l)
    w_in_r = _regroup_weight(
        jnp.swapaxes(w_in, 1, 2), l,
        [(o_g, w_in.shape[2] - o_g), (0, o_wd)] + lora_pieces + [(o_b, o_g - o_b)],
        tk=min(256, d))
    mu = mu_shift[l][None, :]
    mu_l = jnp.concatenate([_pad_cols(mu[:, src:src + width], width + pad)
                            for src, width, pad in zip(lora_src, lora_widths, lora_pads)], axis=1)
    plan = _tile_plan(m, s, w_in_r.shape[1], w_ff1.shape[2])

    z2 = _norm_matmul(h, norm_mix_g[l][None, :], w_in_r, tm=plan["inproj_tm"], tn=plan["inproj_tn"],
                      n_chunks=plan["inproj_norm_chunks"], out_dtype=BF16)
    z3 = z2.reshape(b, s, z2.shape[1])

    ya = _rwkv_branch(
        z3, mu[:, :a_width], mu[:, a_width:2 * a_width], mu[:, 2 * a_width:3 * a_width], mu_l,
        w0[l][None, :], embed_rows(w_up[l], lora_lane[0], windows[0]), a0[l][None, :],
        embed_rows(a_up[l], lora_lane[1], windows[1]), embed_rows(g_up[l], lora_lane[2], windows[2]),
        k_k[l][None, :], k_a[l][None, :], r_k[l].reshape(1, a_width), lnx_g[l][None, :], lnx_b[l][None, :],
        a_width=a_width, rkv_block=rkv_off // a_width, lora_block=lora_off // lora_w, windows=windows,
        ts=plan["rwkv_ts"], out_dtype=BF16)

    ob = _attn_branch(
        z3, _bias_tiles(rel_bias, n_bheads), lambda_q1[l][None, :], lambda_k1[l][None, :],
        lambda_q2[l][None, :], lambda_k2[l][None, :], subln_g[l][:, None],
        q_block=bq_off // LANES, n_heads=n_bheads, lambda_init=lambda_init, out_dtype=BF16)

    h = _merge_out(ya.reshape(m, a_width), ob.reshape(m, b_width), z2, h, p_a[l].astype(BF16),
                   p_b[l].astype(BF16), w_out[l].astype(BF16), gate_block=0, tm=plan["merge_tm"])
    h = _ffn(h, norm_mlp_g[l][None, :], w_ff1[l].astype(BF16), w_ff2[l].astype(BF16),
             norm_final_g[None, :], tm=plan["ffn_tm"], tf=plan["ffn_tf"], n_chunks=plan["ffn_norm_chunks"])
    return h.reshape(b, s, d)
```

```python
import functools
import math

import jax
import jax.numpy as jnp
from jax import lax
from jax.experimental import pallas as pl
from jax.experimental.pallas import tpu as pltpu

F32 = jnp.float32
BF16 = jnp.bfloat16

LANES = 128
HEAD = 64
CHUNK = 64
Q_BLOCK = 128
MXU_DIM = 256
BF16_ROWS = 16
KEY_TILE_BLOCKS = 2
RMS_EPS = 1e-6
GN_EPS = 64e-5
SUBLN_EPS = 1e-5
N_BUCKETS = 32
MAX_DISTANCE = 128
NEG_INF = -1e30
LOG2E = 1.4426950408889634
VMEM_LIMIT = 56 * 1024 * 1024


def _mm(a, b):
    return jnp.dot(a.astype(BF16), b.astype(BF16), preferred_element_type=F32)


def _mm_nt(a, b):
    return lax.dot_general(a.astype(BF16), b.astype(BF16), (((1,), (1,)), ((), ())),
                           preferred_element_type=F32)


def _params(sem):
    return pltpu.CompilerParams(dimension_semantics=sem, vmem_limit_bytes=VMEM_LIMIT)


def _norm_matmul_kernel(x_ref, g_ref, w_ref, o_ref, u_ref):
    @pl.when(pl.program_id(1) == 0)
    def _():
        x = x_ref[...]
        ms = jnp.mean(x * x, axis=-1, keepdims=True)
        u_ref[...] = (x * lax.rsqrt(ms + RMS_EPS) * g_ref[...]).astype(BF16)

    o_ref[...] = jnp.dot(u_ref[...], w_ref[...], preferred_element_type=F32).astype(o_ref.dtype)


def _norm_matmul(x, g, w, *, tm, tn, out_dtype):
    m, d = x.shape
    n = w.shape[1]
    return pl.pallas_call(
        _norm_matmul_kernel,
        grid=(m // tm, n // tn),
        in_specs=[pl.BlockSpec((tm, d), lambda i, j: (i, 0)),
                  pl.BlockSpec((1, d), lambda i, j: (0, 0)),
                  pl.BlockSpec((d, tn), lambda i, j: (0, j))],
        out_specs=pl.BlockSpec((tm, tn), lambda i, j: (i, j)),
        out_shape=jax.ShapeDtypeStruct((m, n), out_dtype),
        scratch_shapes=[pltpu.VMEM((tm, d), BF16)],
        compiler_params=_params(("parallel", "arbitrary")),
        name="norm_matmul",
    )(x, g, w)


def _expand(t, lane_lo):
    zero = jnp.zeros_like(t)
    return jnp.concatenate([jnp.where(lane_lo, t, zero), jnp.where(lane_lo, zero, t)], axis=0)


def _sum_heads(x, head_ones):
    xb = x.astype(BF16)
    g = head_ones.shape[0]
    return jnp.concatenate(
        [jnp.dot(xb[:, i * g:(i + 1) * g], head_ones, preferred_element_type=F32)
         for i in range(x.shape[1] // g)], axis=1)


def _chunk_cumsum(x, tri):
    hi = x.astype(BF16)
    lo = (x - hi.astype(F32)).astype(BF16)
    out = jnp.dot(tri, jnp.concatenate([hi, lo], axis=1), preferred_element_type=F32)
    return out[:, :x.shape[1]] + out[:, x.shape[1]:]


def _rwkv_kernel(zr_ref, zk_ref, zv_ref, zl_ref, mur_ref, muk_ref, muv_ref, mul_ref,
                 w0_ref, wup_ref, a0_ref, aup_ref, gup_ref, kk_ref, ka_ref, rk_ref, lng_ref, lnb_ref,
                 o_ref, cr_ref, ck_ref, cv_ref, cl_ref, h_ref, *, n_chunks, n_pairs, windows):
    c = CHUNK
    c2 = 2 * c
    ts = n_chunks * c
    units = [(p, ch) for ch in range(n_chunks) for p in range(n_pairs)]
    chunks = range(len(units))

    @pl.when(pl.program_id(1) == 0)
    def _():
        cr_ref[...] = jnp.zeros_like(cr_ref)
        ck_ref[...] = jnp.zeros_like(ck_ref)
        cv_ref[...] = jnp.zeros_like(cv_ref)
        cl_ref[...] = jnp.zeros_like(cl_ref)
        h_ref[...] = jnp.zeros_like(h_ref)

    def shifted(z, carry_ref, mu):
        rows = lax.broadcasted_iota(jnp.int32, z.shape, 0)
        prev = jnp.where(rows == 0, carry_ref[...], pltpu.roll(z, 1, 0))
        carry_ref[...] = z[z.shape[0] - 1:, :]
        return z + (prev - z) * mu

    ri = lax.broadcasted_iota(jnp.int32, (c2, c2), 0)
    ci = lax.broadcasted_iota(jnp.int32, (c2, c2), 1)
    strict = ci < ri
    incl = ci <= ri
    eye = (ci == ri).astype(F32)
    hr = lax.broadcasted_iota(jnp.int32, (MXU_DIM, MXU_DIM), 0)
    hc = lax.broadcasted_iota(jnp.int32, (MXU_DIM, MXU_DIM), 1)
    head_ones = ((hr // HEAD) == (hc // HEAD)).astype(BF16)
    ti = lax.broadcasted_iota(jnp.int32, (ts, ts), 0)
    tj = lax.broadcasted_iota(jnp.int32, (ts, ts), 1)
    tri = ((tj <= ti) & ((tj // c) == (ti // c))).astype(BF16)
    lane_lo = lax.broadcasted_iota(jnp.int32, (c, LANES), 1) < HEAD

    r = shifted(zr_ref[0].astype(F32), cr_ref, mur_ref[...])
    k = shifted(zk_ref[0].astype(F32), ck_ref, muk_ref[...])
    v = shifted(zv_ref[0].astype(F32), cv_ref, muv_ref[...])
    lo = shifted(zl_ref[0].astype(F32), cl_ref, mul_ref[...])

    (d0, d1), (a0_, a1_), (g0, g1) = windows
    t_w = w0_ref[...] + _mm(jnp.tanh(lo[:, d0:d1]), wup_ref[...])
    lw = -math.exp(-0.5) * jax.nn.sigmoid(t_w)
    a_sig = jax.nn.sigmoid(a0_ref[...] + _mm(lo[:, a0_:a1_], aup_ref[...]))
    gate = _mm(jax.nn.sigmoid(lo[:, g0:g1]), gup_ref[...])
    kk = k * kk_ref[...]
    kk = kk * lax.rsqrt(jnp.maximum(_sum_heads(kk * kk, head_ones), 1e-24))
    k2 = k * (1.0 + (a_sig - 1.0) * ka_ref[...])
    b_v = kk * a_sig
    bonus = _sum_heads(r * k2 * rk_ref[...], head_ones) * v
    cum = _chunk_cumsum(lw, tri)
    g_in = jnp.exp(cum)
    g_inv = jnp.exp(-cum)
    rt_all = r * g_in
    at_all = -kk * jnp.exp(cum - lw)
    kt_all = k2 * g_inv
    bt_all = b_v * g_inv

    def tile(x, u):
        p, ch = units[u]
        return x[ch * c:(ch + 1) * c, p * LANES:(p + 1) * LANES]

    cum_end = [tile(cum, u)[c - 1:, :] for u in chunks]
    g_tail = [jnp.exp(cum_end[u] - tile(cum, u)) for u in chunks]
    rt = [_expand(tile(rt_all, u), lane_lo) for u in chunks]
    at = [_expand(tile(at_all, u), lane_lo) for u in chunks]
    kt = [_expand(tile(kt_all, u), lane_lo) for u in chunks]
    bt = [_expand(tile(bt_all, u), lane_lo) for u in chunks]
    kb = [_expand(tile(k2, u) * g_tail[u], lane_lo) for u in chunks]
    bb = [_expand(tile(b_v, u) * g_tail[u], lane_lo) for u in chunks]
    ve = [_expand(tile(v, u), lane_lo) for u in chunks]

    sc = [_mm_nt(jnp.concatenate([at[ch], rt[ch]], axis=0), jnp.concatenate([bt[ch], kt[ch]], axis=0))
          for ch in chunks]
    n_ab = [jnp.where(strict, sc[ch][:c2, :c2], 0.0) for ch in chunks]
    a_ak = [jnp.where(strict, sc[ch][:c2, c2:], 0.0) for ch in chunks]
    a_rb = [jnp.where(incl, sc[ch][c2:, :c2], 0.0) for ch in chunks]
    a_rk = [jnp.where(incl, sc[ch][c2:, c2:], 0.0) for ch in chunks]

    p = [eye + n_ab[ch] for ch in chunks]
    nk = [_mm(n_ab[ch], n_ab[ch]) for ch in chunks]
    akv = [_mm(a_ak[ch], ve[ch]) for ch in chunks]
    power = 2
    while power < c:
        if 2 * power < c:
            both = [_mm(nk[ch], jnp.concatenate([nk[ch], p[ch]], axis=1)) for ch in chunks]
            nk = [both[ch][:, :c2] for ch in chunks]
            p = [p[ch] + both[ch][:, c2:] for ch in chunks]
        else:
            p = [p[ch] + _mm(nk[ch], p[ch]) for ch in chunks]
        power *= 2

    ta = [_mm(p[ch], jnp.concatenate([at[ch], akv[ch]], axis=1)) for ch in chunks]
    zero = jnp.zeros((c2, LANES), F32)
    rhs = [jnp.concatenate([ta[ch], jnp.concatenate([zero, ve[ch]], axis=1)], axis=0) for ch in chunks]
    lhs = [jnp.concatenate([jnp.concatenate([a_rb[ch], a_rk[ch]], axis=1),
                            jnp.concatenate([bb[ch].T, kb[ch].T], axis=1)], axis=0) for ch in chunks]
    out = [_mm(lhs[ch], rhs[ch]) for ch in chunks]
    r_hat = [rt[ch] + out[ch][:c2, :LANES] for ch in chunks]
    y_in = [out[ch][:c2, LANES:] for ch in chunks]
    m_state = [eye * jnp.exp(cum_end[ch]) + out[ch][c2:, :LANES] for ch in chunks]
    g_state = [out[ch][c2:, LANES:] for ch in chunks]

    h = [h_ref[p] for p in range(n_pairs)]
    ys = [[] for _ in range(n_pairs)]
    for u in chunks:
        p = units[u][0]
        both = _mm(jnp.concatenate([r_hat[u], m_state[u]], axis=0), h[p])
        ye = both[:c2] + y_in[u]
        h[p] = both[c2:] + g_state[u]
        ys[p].append(ye[:c] + ye[c:])
    for p in range(n_pairs):
        h_ref[p] = h[p]
    y = jnp.concatenate([jnp.concatenate(ys[p], axis=0) for p in range(n_pairs)], axis=1)

    mean = _sum_heads(y, head_ones) * (1.0 / HEAD)
    d = y - mean
    var = _sum_heads(d * d, head_ones) * (1.0 / HEAD)
    yn = d * lax.rsqrt(var + GN_EPS) * lng_ref[...] + lnb_ref[...]
    o_ref[0] = ((yn + bonus) * gate).astype(o_ref.dtype)


def _rwkv_branch(z3, mu_r, mu_k, mu_v, mu_l, w0, wup, a0, aup, gup, k_k, k_a, r_k, lnx_g, lnx_b,
                 *, a_width, rkv_block, lora_block, windows, ts, out_dtype):
    b, s, _ = z3.shape
    n_pairs = a_width // LANES
    lw = mu_l.shape[1]

    def col(blk):
        return lambda bi, si: (bi, si, blk)

    def full(rows, width):
        return pl.BlockSpec((rows, width), lambda bi, si: (0, 0))

    in_specs = [
        pl.BlockSpec((1, ts, a_width), col(rkv_block)),
        pl.BlockSpec((1, ts, a_width), col(rkv_block + 1)),
        pl.BlockSpec((1, ts, a_width), col(rkv_block + 2)),
        pl.BlockSpec((1, ts, lw), col(lora_block)),
        full(1, a_width), full(1, a_width), full(1, a_width), full(1, lw),
        full(1, a_width), full(wup.shape[0], a_width), full(1, a_width), full(aup.shape[0], a_width),
        full(gup.shape[0], a_width),
        full(1, a_width), full(1, a_width), full(1, a_width), full(1, a_width), full(1, a_width),
    ]
    return pl.pallas_call(
        functools.partial(_rwkv_kernel, n_chunks=ts // CHUNK, n_pairs=n_pairs, windows=windows),
        grid=(b, s // ts),
        in_specs=in_specs,
        out_specs=pl.BlockSpec((1, ts, a_width), lambda bi, si: (bi, si, 0)),
        out_shape=jax.ShapeDtypeStruct((b, s, a_width), out_dtype),
        scratch_shapes=[pltpu.VMEM((1, a_width), F32), pltpu.VMEM((1, a_width), F32),
                        pltpu.VMEM((1, a_width), F32), pltpu.VMEM((1, lw), F32),
                        pltpu.VMEM((n_pairs, LANES, LANES), F32)],
        compiler_params=_params(("parallel", "arbitrary")),
        name="rwkv7_chunked",
    )(z3, z3, z3, z3, mu_r, mu_k, mu_v, mu_l, w0, wup, a0, aup, gup, k_k, k_a, r_k, lnx_g, lnx_b)


def _attn_kernel(q_ref, k_ref, v_ref, bias_ref, lq1_ref, lk1_ref, lq2_ref, lk2_ref, sg_ref, o_ref,
                 s_ref, p_ref, vt_ref, *, n_qblocks, lambda_init):
    qr = Q_BLOCK
    lam = (jnp.exp(jnp.sum(lq1_ref[...] * lk1_ref[...])) - jnp.exp(jnp.sum(lq2_ref[...] * lk2_ref[...]))
           + lambda_init)
    lane_lo = lax.broadcasted_iota(jnp.int32, (qr, LANES), 1) < HEAD
    scale = HEAD ** -0.5 * LOG2E
    n_ones = vt_ref.shape[0] - LANES
    n_buf = s_ref.shape[0]

    def key_tiles(qb):
        return [(j, min(KEY_TILE_BLOCKS, qb + 1 - j)) for j in range(0, qb + 1, KEY_TILE_BLOCKS)]

    def logits(qb):
        rows = slice(qb * qr, (qb + 1) * qr)
        vt_ref[:LANES, rows] = v_ref[0, rows, :].astype(F32).T.astype(BF16)
        vt_ref[LANES:, rows] = jnp.ones((n_ones, qr), BF16)
        q = q_ref[0, rows, :].astype(F32) * scale
        qe_t = _expand(q, lane_lo).T.astype(BF16)
        m = None
        for j0, nb in key_tiles(qb):
            s = jnp.dot(k_ref[0, j0 * qr:(j0 + nb) * qr, :], qe_t, preferred_element_type=F32)
            for t in range(nb):
                j = j0 + t
                sj = s[t * qr:(t + 1) * qr, :]
                if qb - j < 2:
                    sj = sj + bias_ref[0, qb - j]
                s_ref[qb % n_buf, j * qr:(j + 1) * qr, :] = sj
                m_j = jnp.max(sj, axis=0, keepdims=True)
                m = m_j if m is None else jnp.maximum(m, m_j)
        return m

    def weighted_values(qb, m):
        for j in range(qb + 1):
            pj = jnp.exp2(s_ref[qb % n_buf, j * qr:(j + 1) * qr, :] - m)
            p_ref[qb % n_buf, j * qr:(j + 1) * qr, :] = pj.astype(BF16)
        n_keys = (qb + 1) * qr
        acc = jnp.dot(vt_ref[:, :n_keys], p_ref[qb % n_buf, :n_keys, :], preferred_element_type=F32)
        o = acc[:LANES] / acc[LANES:LANES + 1]
        o = o[:, :qr] - lam * o[:, qr:]
        o = o * lax.rsqrt(jnp.mean(o * o, axis=0, keepdims=True) + SUBLN_EPS) * sg_ref[...]
        o = o * (1.0 - lambda_init)
        o_ref[0, qb * qr:(qb + 1) * qr, :] = o.T.astype(o_ref.dtype)

    ahead = n_buf - 1
    pending = [logits(qb) for qb in range(min(ahead, n_qblocks))]
    for qb in range(n_qblocks):
        if qb + ahead < n_qblocks:
            pending.append(logits(qb + ahead))
        weighted_values(qb, pending.pop(0))


def _attn_branch(z3, bias, lq1, lk1, lq2, lk2, subln_g, *, q_block, n_heads, lambda_init, out_dtype):
    b, s, _ = z3.shape

    def col(off):
        return lambda bi, hh: (bi, 0, off + hh)

    def small(n):
        return pl.BlockSpec((1, n), lambda bi, hh: (0, 0))

    return pl.pallas_call(
        functools.partial(_attn_kernel, n_qblocks=s // Q_BLOCK, lambda_init=lambda_init),
        grid=(b, n_heads),
        in_specs=[pl.BlockSpec((1, s, LANES), col(q_block)),
                  pl.BlockSpec((1, s, LANES), col(q_block + n_heads)),
                  pl.BlockSpec((1, s, LANES), col(q_block + 2 * n_heads)),
                  pl.BlockSpec((1, 2, Q_BLOCK, 2 * Q_BLOCK), lambda bi, hh: (hh, 0, 0, 0)),
                  small(HEAD), small(HEAD), small(HEAD), small(HEAD),
                  pl.BlockSpec((LANES, 1), lambda bi, hh: (0, 0))],
        out_specs=pl.BlockSpec((1, s, LANES), lambda bi, hh: (bi, 0, hh)),
        out_shape=jax.ShapeDtypeStruct((b, s, n_heads * LANES), out_dtype),
        scratch_shapes=[pltpu.VMEM((4, s, 2 * Q_BLOCK), F32), pltpu.VMEM((4, s, 2 * Q_BLOCK), BF16),
                        pltpu.VMEM((LANES + BF16_ROWS, s), BF16)],
        compiler_params=_params(("parallel", "parallel")),
        name="diff_attention",
    )(z3, z3, z3, bias, lq1, lk1, lq2, lk2, subln_g)


def _t5_bucket(rel):
    nb = N_BUCKETS // 2
    max_exact = nb // 2
    ret = jnp.where(rel > 0, nb, 0)
    n = jnp.abs(rel)
    nf = jnp.maximum(n, 1).astype(jnp.float32)
    large = max_exact + (jnp.log(nf / max_exact) / math.log(MAX_DISTANCE / max_exact)
                         * (nb - max_exact)).astype(jnp.int32)
    large = jnp.minimum(large, nb - 1)
    return ret + jnp.where(n < max_exact, n, large)


def _bias_tiles(rel_bias, n_heads):
    assert Q_BLOCK >= MAX_DISTANCE
    dist = jnp.arange(3)[:, None, None]
    qi = jnp.arange(Q_BLOCK)[None, :, None]
    kj = jnp.arange(Q_BLOCK)[None, None, :]
    bucket = _t5_bucket(kj - qi - dist * Q_BLOCK)
    onehot = (bucket[..., None] == jnp.arange(N_BUCKETS)).astype(F32)
    t = jnp.einsum("dqkn,nh->hdqk", onehot, rel_bias.astype(F32), precision=lax.Precision.HIGHEST)
    allowed = (dist > 0) | ((kj // CHUNK) <= (qi // CHUNK))
    t = jnp.where(allowed[None], t, NEG_INF)
    t = (t[:, :2] - t[:, 2:]) * LOG2E
    t = t.reshape(n_heads, 2, 2, Q_BLOCK, Q_BLOCK).transpose(0, 2, 4, 1, 3)
    return t.reshape(n_heads, 2, Q_BLOCK, 2 * Q_BLOCK)


def _merge_out_kernel(ya_ref, ob_ref, ga_ref, gb_ref, x_ref, pa_ref, pb_ref, wo_ref, o_ref, mg_ref):
    half = mg_ref.shape[1] // 2
    for c0 in (0, half):
        cols = slice(c0, c0 + half)
        oa = jnp.dot(ya_ref[...], pa_ref[:, cols], preferred_element_type=F32)
        ob = jnp.dot(ob_ref[...], pb_ref[:, cols], preferred_element_type=F32)
        mg_ref[:, cols] = (jax.nn.sigmoid(ga_ref[:, cols].astype(F32)) * oa
                           + jax.nn.sigmoid(gb_ref[:, cols].astype(F32)) * ob).astype(BF16)
    o_ref[...] = x_ref[...] + jnp.dot(mg_ref[...], wo_ref[...], preferred_element_type=F32)


def _merge_out(ya, ob, z2, x, p_a, p_b, w_out, *, gate_block, tm):
    m, ka = ya.shape
    kb = ob.shape[1]
    d = w_out.shape[1]

    def resident(shape):
        return pl.BlockSpec(shape, lambda i: (0, 0), pipeline_mode=pl.Buffered(1))

    return pl.pallas_call(
        _merge_out_kernel,
        grid=(m // tm,),
        in_specs=[pl.BlockSpec((tm, ka), lambda i: (i, 0)),
                  pl.BlockSpec((tm, kb), lambda i: (i, 0)),
                  pl.BlockSpec((tm, d), lambda i: (i, gate_block)),
                  pl.BlockSpec((tm, d), lambda i: (i, gate_block + 1)),
                  pl.BlockSpec((tm, d), lambda i: (i, 0)),
                  resident(p_a.shape), resident(p_b.shape), resident(w_out.shape)],
        out_specs=pl.BlockSpec((tm, d), lambda i: (i, 0)),
        out_shape=jax.ShapeDtypeStruct((m, d), F32),
        scratch_shapes=[pltpu.VMEM((tm, d), BF16)],
        compiler_params=_params(("parallel",)),
        name="merge_out_proj",
    )(ya, ob, z2, z2, x, p_a, p_b, w_out)


def _ffn_kernel(h_ref, g1_ref, w1_ref, w2_ref, g2_ref, o_ref, m_ref):
    j = pl.program_id(1)

    @pl.when(j == 0)
    def _():
        x = h_ref[...]
        ms = jnp.mean(x * x, axis=-1, keepdims=True)
        m_ref[...] = (x * lax.rsqrt(ms + RMS_EPS) * g1_ref[...]).astype(BF16)
        o_ref[...] = x

    f = jnp.dot(m_ref[...], w1_ref[...], preferred_element_type=F32)
    f = jnp.square(jnp.maximum(f, 0.0)).astype(BF16)
    o_ref[...] += jnp.dot(f, w2_ref[...], preferred_element_type=F32)

    @pl.when(j == pl.num_programs(1) - 1)
    def _():
        y = o_ref[...]
        ms = jnp.mean(y * y, axis=-1, keepdims=True)
        o_ref[...] = y * lax.rsqrt(ms + RMS_EPS) * g2_ref[...]


def _ffn(h, g1, w1, w2, g2, *, tm, tf):
    m, d = h.shape
    dff = w1.shape[1]
    return pl.pallas_call(
        _ffn_kernel,
        grid=(m // tm, dff // tf),
        in_specs=[pl.BlockSpec((tm, d), lambda i, j: (i, 0)),
                  pl.BlockSpec((1, d), lambda i, j: (0, 0)),
                  pl.BlockSpec((d, tf), lambda i, j: (0, j)),
                  pl.BlockSpec((tf, d), lambda i, j: (j, 0)),
                  pl.BlockSpec((1, d), lambda i, j: (0, 0))],
        out_specs=pl.BlockSpec((tm, d), lambda i, j: (i, 0)),
        out_shape=jax.ShapeDtypeStruct((m, d), F32),
        scratch_shapes=[pltpu.VMEM((tm, d), BF16)],
        compiler_params=_params(("parallel", "arbitrary")),
        name="ffn_residual_norm",
    )(h, g1, w1, w2, g2)


def _pad_cols(w, width):
    return jnp.pad(w, ((0, 0), (0, width - w.shape[1])))


def _regroup_kernel(wt_ref, o_ref, *, pieces):
    col = 0
    for src, width in pieces:
        for c0 in range(0, width, LANES):
            cw = min(LANES, width - c0)
            if src is None:
                o_ref[:, col + c0:col + c0 + cw] = jnp.zeros((o_ref.shape[0], cw), o_ref.dtype)
            else:
                assert src + c0 + LANES <= wt_ref.shape[1]
                blk = wt_ref[0, src + c0:src + c0 + LANES, :].T
                o_ref[:, col + c0:col + c0 + cw] = blk[:, :cw].astype(o_ref.dtype)
        col += width


def _regroup_weight(wt, layer, pieces, *, tk):
    _, n, k = wt.shape
    out_cols = sum(width for _, width in pieces)
    return pl.pallas_call(
        functools.partial(_regroup_kernel, pieces=pieces),
        grid=(k // tk,),
        in_specs=[pl.BlockSpec((1, n, tk), lambda i: (layer, 0, i))],
        out_specs=pl.BlockSpec((tk, out_cols), lambda i: (i, 0)),
        out_shape=jax.ShapeDtypeStruct((k, out_cols), BF16),
        compiler_params=_params(("parallel",)),
        name="regroup_weight",
    )(wt)


def _tile_plan(m, s, z_cols, d_ff):
    plan = dict(
        inproj_tm=min(1024, m), inproj_tn=1536,
        rwkv_ts=4 * CHUNK,
        merge_tm=min(512, m),
        ffn_tm=min(512, m), ffn_tf=1024)
    assert m % plan["inproj_tm"] == 0 and z_cols % plan["inproj_tn"] == 0
    assert s % plan["rwkv_ts"] == 0 and s % (2 * Q_BLOCK) == 0
    assert m % plan["merge_tm"] == 0 and m % plan["ffn_tm"] == 0 and d_ff % plan["ffn_tf"] == 0
    return plan


def kernel(x, norm_mix_g, w_in, mu_shift, w0, w_up, a0, a_up, g_up, k_k, k_a, r_k, lnx_g, lnx_b, lambda_q1, lambda_k1, lambda_q2, lambda_k2, subln_g, rel_bias, p_a, p_b, w_out, norm_mlp_g, w_ff1, w_ff2, norm_final_g):
    b, s, d = x.shape
    depth = w_in.shape[0]
    a_width = w0.shape[1]
    d_lora, a_lora, g_lora = w_up.shape[1], a_up.shape[1], g_up.shape[1]
    n_bheads = rel_bias.shape[1] // 2
    b_width = n_bheads * LANES
    assert depth == 1, "the final norm is fused into the (single) layer's MLP kernel"
    assert a_width % MXU_DIM == 0 and subln_g.shape[1] == LANES
    m = b * s

    o_wd = 3 * a_width
    o_b = o_wd + d_lora + a_lora + g_lora
    o_g = o_b + 3 * b_width
    def lanes_up(n):
        return -(-n // LANES) * LANES

    lora_widths = (d_lora, a_lora, g_lora)
    lora_src = (o_wd, o_wd + d_lora, o_wd + d_lora + a_lora)
    lora_lane = (0, lanes_up(d_lora), lanes_up(d_lora) + lanes_up(a_lora))
    lora_used = lora_lane[2] + lanes_up(g_lora)
    lora_w = -(-lora_used // (4 * LANES)) * (4 * LANES)
    lora_pads = (lanes_up(d_lora) - d_lora, lanes_up(a_lora) - a_lora, lora_w - lora_lane[2] - g_lora)
    rkv_off = 2 * d
    lora_off = rkv_off + 3 * a_width
    bq_off = lora_off + lora_w
    assert lora_off % lora_w == 0

    def embed_rows(w, off, win):
        return jnp.pad(w, ((off - win[0], win[1] - off - w.shape[0]), (0, 0))).astype(BF16)

    windows = tuple((off, off + lanes_up(width)) for off, width in zip(lora_lane, lora_widths))
    lora_pieces = [piece for src, width, pad in zip(lora_src, lora_widths, lora_pads)
                   for piece in ((src, width), (None, pad)) if piece[1]]

    h = x.reshape(m, d)
    l = 0
    lambda_init = 0.8 - 0.6 * math.exp(-0.3 * l)
    w_in_r = _regroup_weight(
        jnp.swapaxes(w_in, 1, 2), l,
        [(o_g, w_in.shape[2] - o_g), (0, o_wd)] + lora_pieces + [(o_b, o_g - o_b)],
        tk=min(256, d))
    mu = mu_shift[l][None, :]
    mu_l = jnp.concatenate([_pad_cols(mu[:, src:src + width], width + pad)
                            for src, width, pad in zip(lora_src, lora_widths, lora_pads)], axis=1)
    plan = _tile_plan(m, s, w_in_r.shape[1], w_ff1.shape[2])

    z2 = _norm_matmul(h, norm_mix_g[l][None, :], w_in_r, tm=plan["inproj_tm"], tn=plan["inproj_tn"],
                      out_dtype=BF16)
    z3 = z2.reshape(b, s, z2.shape[1])

    ya = _rwkv_branch(
        z3, mu[:, :a_width], mu[:, a_width:2 * a_width], mu[:, 2 * a_width:3 * a_width], mu_l,
        w0[l][None, :], embed_rows(w_up[l], lora_lane[0], windows[0]), a0[l][None, :],
        embed_rows(a_up[l], lora_lane[1], windows[1]), embed_rows(g_up[l], lora_lane[2], windows[2]),
        k_k[l][None, :], k_a[l][None, :], r_k[l].reshape(1, a_width), lnx_g[l][None, :], lnx_b[l][None, :],
        a_width=a_width, rkv_block=rkv_off // a_width, lora_block=lora_off // lora_w, windows=windows,
        ts=plan["rwkv_ts"], out_dtype=BF16)

    ob = _attn_branch(
        z3, _bias_tiles(rel_bias, n_bheads), lambda_q1[l][None, :], lambda_k1[l][None, :],
        lambda_q2[l][None, :], lambda_k2[l][None, :], subln_g[l][:, None],
        q_block=bq_off // LANES, n_heads=n_bheads, lambda_init=lambda_init, out_dtype=BF16)

    h = _merge_out(ya.reshape(m, a_width), ob.reshape(m, b_width), z2, h, p_a[l].astype(BF16),
                   p_b[l].astype(BF16), w_out[l].astype(BF16), gate_block=0, tm=plan["merge_tm"])
    h = _ffn(h, norm_mlp_g[l][None, :], w_ff1[l].astype(BF16), w_ff2[l].astype(BF16),
             norm_final_g[None, :], tm=plan["ffn_tm"], tf=plan["ffn_tf"])
    return h.reshape(b, s, d)
```

```python
import functools
import math

import jax
import jax.numpy as jnp
from jax import lax
from jax.experimental import pallas as pl
from jax.experimental.pallas import tpu as pltpu

F32 = jnp.float32
BF16 = jnp.bfloat16

LANES = 128
HEAD = 64
CHUNK = 64
Q_BLOCK = 128
MXU_DIM = 256
BF16_ROWS = 16
KEY_TILE_BLOCKS = 2
RMS_EPS = 1e-6
GN_EPS = 64e-5
SUBLN_EPS = 1e-5
N_BUCKETS = 32
MAX_DISTANCE = 128
NEG_INF = -1e30
LOG2E = 1.4426950408889634
VMEM_LIMIT = 56 * 1024 * 1024


def _mm(a, b):
    return jnp.dot(a.astype(BF16), b.astype(BF16), preferred_element_type=F32)


def _mm_nt(a, b):
    return lax.dot_general(a.astype(BF16), b.astype(BF16), (((1,), (1,)), ((), ())),
                           preferred_element_type=F32)


def _params(sem):
    return pltpu.CompilerParams(dimension_semantics=sem, vmem_limit_bytes=VMEM_LIMIT)


def _norm_matmul_kernel(x_ref, g_ref, w_ref, o_ref, u_ref):
    @pl.when(pl.program_id(1) == 0)
    def _():
        x = x_ref[...]
        ms = jnp.mean(x * x, axis=-1, keepdims=True)
        u_ref[...] = (x * lax.rsqrt(ms + RMS_EPS) * g_ref[...]).astype(BF16)

    o_ref[...] = jnp.dot(u_ref[...], w_ref[...], preferred_element_type=F32).astype(o_ref.dtype)


def _norm_matmul(x, g, w, *, tm, tn, out_dtype):
    m, d = x.shape
    n = w.shape[1]
    return pl.pallas_call(
        _norm_matmul_kernel,
        grid=(m // tm, n // tn),
        in_specs=[pl.BlockSpec((tm, d), lambda i, j: (i, 0)),
                  pl.BlockSpec((1, d), lambda i, j: (0, 0)),
                  pl.BlockSpec((d, tn), lambda i, j: (0, j))],
        out_specs=pl.BlockSpec((tm, tn), lambda i, j: (i, j)),
        out_shape=jax.ShapeDtypeStruct((m, n), out_dtype),
        scratch_shapes=[pltpu.VMEM((tm, d), BF16)],
        compiler_params=_params(("parallel", "arbitrary")),
        name="norm_matmul",
    )(x, g, w)


def _expand(t, lane_lo):
    zero = jnp.zeros_like(t)
    return jnp.concatenate([jnp.where(lane_lo, t, zero), jnp.where(lane_lo, zero, t)], axis=0)


def _sum_heads(x, head_ones):
    xb = x.astype(BF16)
    g = head_ones.shape[0]
    return jnp.concatenate(
        [jnp.dot(xb[:, i * g:(i + 1) * g], head_ones, preferred_element_type=F32)
         for i in range(x.shape[1] // g)], axis=1)


def _chunk_cumsum(x, tri):
    hi = x.astype(BF16)
    lo = (x - hi.astype(F32)).astype(BF16)
    out = jnp.dot(tri, jnp.concatenate([hi, lo], axis=1), preferred_element_type=F32)
    return out[:, :x.shape[1]] + out[:, x.shape[1]:]


def _rwkv_kernel(zr_ref, zk_ref, zv_ref, zl_ref, mur_ref, muk_ref, muv_ref, mul_ref,
                 w0_ref, wup_ref, a0_ref, aup_ref, gup_ref, kk_ref, ka_ref, rk_ref, lng_ref, lnb_ref,
                 o_ref, cr_ref, ck_ref, cv_ref, cl_ref, h_ref, *, n_chunks, n_pairs, windows):
    c = CHUNK
    c2 = 2 * c
    ts = n_chunks * c
    units = [(p, ch) for ch in range(n_chunks) for p in range(n_pairs)]
    chunks = range(len(units))

    @pl.when(pl.program_id(1) == 0)
    def _():
        cr_ref[...] = jnp.zeros_like(cr_ref)
        ck_ref[...] = jnp.zeros_like(ck_ref)
        cv_ref[...] = jnp.zeros_like(cv_ref)
        cl_ref[...] = jnp.zeros_like(cl_ref)
        h_ref[...] = jnp.zeros_like(h_ref)

    def shifted(z, carry_ref, mu):
        rows = lax.broadcasted_iota(jnp.int32, z.shape, 0)
        prev = jnp.where(rows == 0, carry_ref[...], pltpu.roll(z, 1, 0))
        carry_ref[...] = z[z.shape[0] - 1:, :]
        return z + (prev - z) * mu

    ri = lax.broadcasted_iota(jnp.int32, (c2, c2), 0)
    ci = lax.broadcasted_iota(jnp.int32, (c2, c2), 1)
    strict = ci < ri
    incl = ci <= ri
    eye = (ci == ri).astype(F32)
    hr = lax.broadcasted_iota(jnp.int32, (MXU_DIM, MXU_DIM), 0)
    hc = lax.broadcasted_iota(jnp.int32, (MXU_DIM, MXU_DIM), 1)
    head_ones = ((hr // HEAD) == (hc // HEAD)).astype(BF16)
    ti = lax.broadcasted_iota(jnp.int32, (ts, ts), 0)
    tj = lax.broadcasted_iota(jnp.int32, (ts, ts), 1)
    tri = ((tj <= ti) & ((tj // c) == (ti // c))).astype(BF16)
    lane_lo = lax.broadcasted_iota(jnp.int32, (c, LANES), 1) < HEAD

    r = shifted(zr_ref[0].astype(F32), cr_ref, mur_ref[...])
    k = shifted(zk_ref[0].astype(F32), ck_ref, muk_ref[...])
    v = shifted(zv_ref[0].astype(F32), cv_ref, muv_ref[...])
    lo = shifted(zl_ref[0].astype(F32), cl_ref, mul_ref[...])

    (d0, d1), (a0_, a1_), (g0, g1) = windows
    t_w = w0_ref[...] + _mm(jnp.tanh(lo[:, d0:d1]), wup_ref[...])
    lw = -math.exp(-0.5) * jax.nn.sigmoid(t_w)
    a_sig = jax.nn.sigmoid(a0_ref[...] + _mm(lo[:, a0_:a1_], aup_ref[...]))
    gate = _mm(jax.nn.sigmoid(lo[:, g0:g1]), gup_ref[...])
    kk = k * kk_ref[...]
    kk = kk * lax.rsqrt(jnp.maximum(_sum_heads(kk * kk, head_ones), 1e-24))
    k2 = k * (1.0 + (a_sig - 1.0) * ka_ref[...])
    b_v = kk * a_sig
    bonus = _sum_heads(r * k2 * rk_ref[...], head_ones) * v
    cum = _chunk_cumsum(lw, tri)
    g_in = jnp.exp(cum)
    g_inv = jnp.exp(-cum)
    rt_all = r * g_in
    at_all = -kk * jnp.exp(cum - lw)
    kt_all = k2 * g_inv
    bt_all = b_v * g_inv

    def tile(x, u):
        p, ch = units[u]
        return x[ch * c:(ch + 1) * c, p * LANES:(p + 1) * LANES]

    cum_end = [tile(cum, u)[c - 1:, :] for u in chunks]
    g_tail = [jnp.exp(cum_end[u] - tile(cum, u)) for u in chunks]
    rt = [_expand(tile(rt_all, u), lane_lo) for u in chunks]
    at = [_expand(tile(at_all, u), lane_lo) for u in chunks]
    kt = [_expand(tile(kt_all, u), lane_lo) for u in chunks]
    bt = [_expand(tile(bt_all, u), lane_lo) for u in chunks]
    kb = [_expand(tile(k2, u) * g_tail[u], lane_lo) for u in chunks]
    bb = [_expand(tile(b_v, u) * g_tail[u], lane_lo) for u in chunks]
    ve = [_expand(tile(v, u), lane_lo) for u in chunks]

    sc = [_mm_nt(jnp.concatenate([at[ch], rt[ch]], axis=0), jnp.concatenate([bt[ch], kt[ch]], axis=0))
          for ch in chunks]
    n_ab = [jnp.where(strict, sc[ch][:c2, :c2], 0.0) for ch in chunks]
    a_ak = [jnp.where(strict, sc[ch][:c2, c2:], 0.0) for ch in chunks]
    a_rb = [jnp.where(incl, sc[ch][c2:, :c2], 0.0) for ch in chunks]
    a_rk = [jnp.where(incl, sc[ch][c2:, c2:], 0.0) for ch in chunks]

    p = [eye + n_ab[ch] for ch in chunks]
    nk = [_mm(n_ab[ch], n_ab[ch]) for ch in chunks]
    akv = [_mm(a_ak[ch], ve[ch]) for ch in chunks]
    power = 2
    while power < c:
        if 2 * power < c:
            both = [_mm(nk[ch], jnp.concatenate([nk[ch], p[ch]], axis=1)) for ch in chunks]
            nk = [both[ch][:, :c2] for ch in chunks]
            p = [p[ch] + both[ch][:, c2:] for ch in chunks]
        else:
            p = [p[ch] + _mm(nk[ch], p[ch]) for ch in chunks]
        power *= 2

    ta = [_mm(p[ch], jnp.concatenate([at[ch], akv[ch]], axis=1)) for ch in chunks]
    zero = jnp.zeros((c2, LANES), F32)
    rhs = [jnp.concatenate([ta[ch], jnp.concatenate([zero, ve[ch]], axis=1)], axis=0) for ch in chunks]
    lhs = [jnp.concatenate([jnp.concatenate([a_rb[ch], a_rk[ch]], axis=1),
                            jnp.concatenate([bb[ch].T, kb[ch].T], axis=1)], axis=0) for ch in chunks]
    out = [_mm(lhs[ch], rhs[ch]) for ch in chunks]
    r_hat = [rt[ch] + out[ch][:c2, :LANES] for ch in chunks]
    y_in = [out[ch][:c2, LANES:] for ch in chunks]
    m_state = [eye * jnp.exp(cum_end[ch]) + out[ch][c2:, :LANES] for ch in chunks]
    g_state = [out[ch][c2:, LANES:] for ch in chunks]

    h = [h_ref[p] for p in range(n_pairs)]
    ys = [[] for _ in range(n_pairs)]
    for u in chunks:
        p = units[u][0]
        both = _mm(jnp.concatenate([r_hat[u], m_state[u]], axis=0), h[p])
        ye = both[:c2] + y_in[u]
        h[p] = both[c2:] + g_state[u]
        ys[p].append(ye[:c] + ye[c:])
    for p in range(n_pairs):
        h_ref[p] = h[p]
    y = jnp.concatenate([jnp.concatenate(ys[p], axis=0) for p in range(n_pairs)], axis=1)

    mean = _sum_heads(y, head_ones) * (1.0 / HEAD)
    d = y - mean
    var = _sum_heads(d * d, head_ones) * (1.0 / HEAD)
    yn = d * lax.rsqrt(var + GN_EPS) * lng_ref[...] + lnb_ref[...]
    o_ref[0] = ((yn + bonus) * gate).astype(o_ref.dtype)


def _rwkv_branch(z3, mu_r, mu_k, mu_v, mu_l, w0, wup, a0, aup, gup, k_k, k_a, r_k, lnx_g, lnx_b,
                 *, a_width, rkv_block, lora_block, windows, ts, out_dtype):
    b, s, _ = z3.shape
    n_pairs = a_width // LANES
    lw = mu_l.shape[1]

    def col(blk):
        return lambda bi, si: (bi, si, blk)

    def full(rows, width):
        return pl.BlockSpec((rows, width), lambda bi, si: (0, 0))

    in_specs = [
        pl.BlockSpec((1, ts, a_width), col(rkv_block)),
        pl.BlockSpec((1, ts, a_width), col(rkv_block + 1)),
        pl.BlockSpec((1, ts, a_width), col(rkv_block + 2)),
        pl.BlockSpec((1, ts, lw), col(lora_block)),
        full(1, a_width), full(1, a_width), full(1, a_width), full(1, lw),
        full(1, a_width), full(wup.shape[0], a_width), full(1, a_width), full(aup.shape[0], a_width),
        full(gup.shape[0], a_width),
        full(1, a_width), full(1, a_width), full(1, a_width), full(1, a_width), full(1, a_width),
    ]
    return pl.pallas_call(
        functools.partial(_rwkv_kernel, n_chunks=ts // CHUNK, n_pairs=n_pairs, windows=windows),
        grid=(b, s // ts),
        in_specs=in_specs,
        out_specs=pl.BlockSpec((1, ts, a_width), lambda bi, si: (bi, si, 0)),
        out_shape=jax.ShapeDtypeStruct((b, s, a_width), out_dtype),
        scratch_shapes=[pltpu.VMEM((1, a_width), F32), pltpu.VMEM((1, a_width), F32),
                        pltpu.VMEM((1, a_width), F32), pltpu.VMEM((1, lw), F32),
                        pltpu.VMEM((n_pairs, LANES, LANES), F32)],
        compiler_params=_params(("parallel", "arbitrary")),
        name="rwkv7_chunked",
    )(z3, z3, z3, z3, mu_r, mu_k, mu_v, mu_l, w0, wup, a0, aup, gup, k_k, k_a, r_k, lnx_g, lnx_b)


def _attn_kernel(q_ref, k_ref, v_ref, bias_ref, lq1_ref, lk1_ref, lq2_ref, lk2_ref, sg_ref, o_ref,
                 s_ref, p_ref, vt_ref, *, n_qblocks, lambda_init):
    qr = Q_BLOCK
    lam = (jnp.exp(jnp.sum(lq1_ref[...] * lk1_ref[...])) - jnp.exp(jnp.sum(lq2_ref[...] * lk2_ref[...]))
           + lambda_init)
    lane_lo = lax.broadcasted_iota(jnp.int32, (qr, LANES), 1) < HEAD
    scale = HEAD ** -0.5 * LOG2E
    n_ones = vt_ref.shape[0] - LANES
    n_buf = s_ref.shape[0]

    def key_tiles(qb):
        return [(j, min(KEY_TILE_BLOCKS, qb + 1 - j)) for j in range(0, qb + 1, KEY_TILE_BLOCKS)]

    def logits(qb):
        rows = slice(qb * qr, (qb + 1) * qr)
        vt_ref[:LANES, rows] = v_ref[0, rows, :].astype(F32).T.astype(BF16)
        vt_ref[LANES:, rows] = jnp.ones((n_ones, qr), BF16)
        q = q_ref[0, rows, :].astype(F32) * scale
        qe_t = _expand(q, lane_lo).T.astype(BF16)
        m = None
        for j0, nb in key_tiles(qb):
            s = jnp.dot(k_ref[0, j0 * qr:(j0 + nb) * qr, :], qe_t, preferred_element_type=F32)
            for t in range(nb):
                j = j0 + t
                sj = s[t * qr:(t + 1) * qr, :]
                if qb - j < 2:
                    sj = sj + bias_ref[0, qb - j]
                s_ref[qb % n_buf, j * qr:(j + 1) * qr, :] = sj
                m_j = jnp.max(sj, axis=0, keepdims=True)
                m = m_j if m is None else jnp.maximum(m, m_j)
        return m

    def weighted_values(qb, m):
        for j in range(qb + 1):
            pj = jnp.exp2(s_ref[qb % n_buf, j * qr:(j + 1) * qr, :] - m)
            p_ref[qb % n_buf, j * qr:(j + 1) * qr, :] = pj.astype(BF16)
        n_keys = (qb + 1) * qr
        acc = jnp.dot(vt_ref[:, :n_keys], p_ref[qb % n_buf, :n_keys, :], preferred_element_type=F32)
        o = acc[:LANES] / acc[LANES:LANES + 1]
        o = o[:, :qr] - lam * o[:, qr:]
        o = o * lax.rsqrt(jnp.mean(o * o, axis=0, keepdims=True) + SUBLN_EPS) * sg_ref[...]
        o = o * (1.0 - lambda_init)
        o_ref[0, qb * qr:(qb + 1) * qr, :] = o.T.astype(o_ref.dtype)

    ahead = n_buf - 1
    pending = [logits(qb) for qb in range(min(ahead, n_qblocks))]
    for qb in range(n_qblocks):
        if qb + ahead < n_qblocks:
            pending.append(logits(qb + ahead))
        weighted_values(qb, pending.pop(0))


def _attn_branch(z3, bias, lq1, lk1, lq2, lk2, subln_g, *, q_block, n_heads, lambda_init, out_dtype):
    b, s, _ = z3.shape

    def col(off):
        return lambda bi, hh: (bi, 0, off + hh)

    def small(n):
        return pl.BlockSpec((1, n), lambda bi, hh: (0, 0))

    return pl.pallas_call(
        functools.partial(_attn_kernel, n_qblocks=s // Q_BLOCK, lambda_init=lambda_init),
        grid=(b, n_heads),
        in_specs=[pl.BlockSpec((1, s, LANES), col(q_block)),
                  pl.BlockSpec((1, s, LANES), col(q_block + n_heads)),
                  pl.BlockSpec((1, s, LANES), col(q_block + 2 * n_heads)),
                  pl.BlockSpec((1, 2, Q_BLOCK, 2 * Q_BLOCK), lambda bi, hh: (hh, 0, 0, 0)),
                  small(HEAD), small(HEAD), small(HEAD), small(HEAD),
                  pl.BlockSpec((LANES, 1), lambda bi, hh: (0, 0))],
        out_specs=pl.BlockSpec((1, s, LANES), lambda bi, hh: (bi, 0, hh)),
        out_shape=jax.ShapeDtypeStruct((b, s, n_heads * LANES), out_dtype),
        scratch_shapes=[pltpu.VMEM((2, s, 2 * Q_BLOCK), F32), pltpu.VMEM((2, s, 2 * Q_BLOCK), BF16),
                        pltpu.VMEM((LANES + BF16_ROWS, s), BF16)],
        compiler_params=_params(("parallel", "parallel")),
        name="diff_attention",
    )(z3, z3, z3, bias, lq1, lk1, lq2, lk2, subln_g)


def _t5_bucket(rel):
    nb = N_BUCKETS // 2
    max_exact = nb // 2
    ret = jnp.where(rel > 0, nb, 0)
    n = jnp.abs(rel)
    nf = jnp.maximum(n, 1).astype(jnp.float32)
    large = max_exact + (jnp.log(nf / max_exact) / math.log(MAX_DISTANCE / max_exact)
                         * (nb - max_exact)).astype(jnp.int32)
    large = jnp.minimum(large, nb - 1)
    return ret + jnp.where(n < max_exact, n, large)


def _bias_tiles(rel_bias, n_heads):
    assert Q_BLOCK >= MAX_DISTANCE
    dist = jnp.arange(3)[:, None, None]
    qi = jnp.arange(Q_BLOCK)[None, :, None]
    kj = jnp.arange(Q_BLOCK)[None, None, :]
    bucket = _t5_bucket(kj - qi - dist * Q_BLOCK)
    onehot = (bucket[..., None] == jnp.arange(N_BUCKETS)).astype(F32)
    t = jnp.einsum("dqkn,nh->hdqk", onehot, rel_bias.astype(F32), precision=lax.Precision.HIGHEST)
    allowed = (dist > 0) | ((kj // CHUNK) <= (qi // CHUNK))
    t = jnp.where(allowed[None], t, NEG_INF)
    t = (t[:, :2] - t[:, 2:]) * LOG2E
    t = t.reshape(n_heads, 2, 2, Q_BLOCK, Q_BLOCK).transpose(0, 2, 4, 1, 3)
    return t.reshape(n_heads, 2, Q_BLOCK, 2 * Q_BLOCK)


def _merge_out_kernel(ya_ref, ob_ref, ga_ref, gb_ref, x_ref, pa_ref, pb_ref, wo_ref, o_ref, mg_ref):
    half = mg_ref.shape[1] // 2
    for c0 in (0, half):
        cols = slice(c0, c0 + half)
        oa = jnp.dot(ya_ref[...], pa_ref[:, cols], preferred_element_type=F32)
        ob = jnp.dot(ob_ref[...], pb_ref[:, cols], preferred_element_type=F32)
        mg_ref[:, cols] = (jax.nn.sigmoid(ga_ref[:, cols].astype(F32)) * oa
                           + jax.nn.sigmoid(gb_ref[:, cols].astype(F32)) * ob).astype(BF16)
    o_ref[...] = x_ref[...] + jnp.dot(mg_ref[...], wo_ref[...], preferred_element_type=F32)


def _merge_out(ya, ob, z2, x, p_a, p_b, w_out, *, gate_block, tm):
    m, ka = ya.shape
    kb = ob.shape[1]
    d = w_out.shape[1]

    def resident(shape):
        return pl.BlockSpec(shape, lambda i: (0, 0), pipeline_mode=pl.Buffered(1))

    return pl.pallas_call(
        _merge_out_kernel,
        grid=(m // tm,),
        in_specs=[pl.BlockSpec((tm, ka), lambda i: (i, 0)),
                  pl.BlockSpec((tm, kb), lambda i: (i, 0)),
                  pl.BlockSpec((tm, d), lambda i: (i, gate_block)),
                  pl.BlockSpec((tm, d), lambda i: (i, gate_block + 1)),
                  pl.BlockSpec((tm, d), lambda i: (i, 0)),
                  resident(p_a.shape), resident(p_b.shape), resident(w_out.shape)],
        out_specs=pl.BlockSpec((tm, d), lambda i: (i, 0)),
        out_shape=jax.ShapeDtypeStruct((m, d), F32),
        scratch_shapes=[pltpu.VMEM((tm, d), BF16)],
        compiler_params=_params(("parallel",)),
        name="merge_out_proj",
    )(ya, ob, z2, z2, x, p_a, p_b, w_out)


def _ffn_kernel(h_ref, g1_ref, w1_ref, w2_ref, g2_ref, o_ref, m_ref):
    j = pl.program_id(1)

    @pl.when(j == 0)
    def _():
        x = h_ref[...]
        ms = jnp.mean(x * x, axis=-1, keepdims=True)
        m_ref[...] = (x * lax.rsqrt(ms + RMS_EPS) * g1_ref[...]).astype(BF16)
        o_ref[...] = x

    f = jnp.dot(m_ref[...], w1_ref[...], preferred_element_type=F32)
    f = jnp.square(jnp.maximum(f, 0.0)).astype(BF16)
    o_ref[...] += jnp.dot(f, w2_ref[...], preferred_element_type=F32)

    @pl.when(j == pl.num_programs(1) - 1)
    def _():
        y = o_ref[...]
        ms = jnp.mean(y * y, axis=-1, keepdims=True)
        o_ref[...] = y * lax.rsqrt(ms + RMS_EPS) * g2_ref[...]


def _ffn(h, g1, w1, w2, g2, *, tm, tf):
    m, d = h.shape
    dff = w1.shape[1]
    return pl.pallas_call(
        _ffn_kernel,
        grid=(m // tm, dff // tf),
        in_specs=[pl.BlockSpec((tm, d), lambda i, j: (i, 0)),
                  pl.BlockSpec((1, d), lambda i, j: (0, 0)),
                  pl.BlockSpec((d, tf), lambda i, j: (0, j)),
                  pl.BlockSpec((tf, d), lambda i, j: (j, 0)),
                  pl.BlockSpec((1, d), lambda i, j: (0, 0))],
        out_specs=pl.BlockSpec((tm, d), lambda i, j: (i, 0)),
        out_shape=jax.ShapeDtypeStruct((m, d), F32),
        scratch_shapes=[pltpu.VMEM((tm, d), BF16)],
        compiler_params=_params(("parallel", "arbitrary")),
        name="ffn_residual_norm",
    )(h, g1, w1, w2, g2)


def _pad_cols(w, width):
    return jnp.pad(w, ((0, 0), (0, width - w.shape[1])))


def _regroup_kernel(wt_ref, o_ref, *, pieces):
    col = 0
    for src, width in pieces:
        for c0 in range(0, width, LANES):
            cw = min(LANES, width - c0)
            if src is None:
                o_ref[:, col + c0:col + c0 + cw] = jnp.zeros((o_ref.shape[0], cw), o_ref.dtype)
            else:
                assert src + c0 + LANES <= wt_ref.shape[1]
                blk = wt_ref[0, src + c0:src + c0 + LANES, :].T
                o_ref[:, col + c0:col + c0 + cw] = blk[:, :cw].astype(o_ref.dtype)
        col += width


def _regroup_weight(wt, layer, pieces, *, tk):
    _, n, k = wt.shape
    out_cols = sum(width for _, width in pieces)
    return pl.pallas_call(
        functools.partial(_regroup_kernel, pieces=pieces),
        grid=(k // tk,),
        in_specs=[pl.BlockSpec((1, n, tk), lambda i: (layer, 0, i))],
        out_specs=pl.BlockSpec((tk, out_cols), lambda i: (i, 0)),
        out_shape=jax.ShapeDtypeStruct((k, out_cols), BF16),
        compiler_params=_params(("parallel",)),
        name="regroup_weight",
    )(wt)


def _tile_plan(m, s, z_cols, d_ff):
    plan = dict(
        inproj_tm=min(1024, m), inproj_tn=1792,
        rwkv_ts=4 * CHUNK,
        merge_tm=min(512, m),
        ffn_tm=min(512, m), ffn_tf=1024)
    assert m % plan["inproj_tm"] == 0 and z_cols % plan["inproj_tn"] == 0
    assert s % plan["rwkv_ts"] == 0 and s % (2 * Q_BLOCK) == 0
    assert m % plan["merge_tm"] == 0 and m % plan["ffn_tm"] == 0 and d_ff % plan["ffn_tf"] == 0
    return plan


def kernel(x, norm_mix_g, w_in, mu_shift, w0, w_up, a0, a_up, g_up, k_k, k_a, r_k, lnx_g, lnx_b, lambda_q1, lambda_k1, lambda_q2, lambda_k2, subln_g, rel_bias, p_a, p_b, w_out, norm_mlp_g, w_ff1, w_ff2, norm_final_g):
    b, s, d = x.shape
    depth = w_in.shape[0]
    a_width = w0.shape[1]
    d_lora, a_lora, g_lora = w_up.shape[1], a_up.shape[1], g_up.shape[1]
    n_bheads = rel_bias.shape[1] // 2
    b_width = n_bheads * LANES
    assert depth == 1, "the final norm is fused into the (single) layer's MLP kernel"
    assert a_width % MXU_DIM == 0 and subln_g.shape[1] == LANES
    m = b * s

    o_wd = 3 * a_width
    o_b = o_wd + d_lora + a_lora + g_lora
    o_g = o_b + 3 * b_width
    def lanes_up(n):
        return -(-n // LANES) * LANES

    lora_widths = (d_lora, a_lora, g_lora)
    lora_src = (o_wd, o_wd + d_lora, o_wd + d_lora + a_lora)
    lora_lane = (0, lanes_up(d_lora), lanes_up(d_lora) + lanes_up(a_lora))
    lora_used = lora_lane[2] + lanes_up(g_lora)
    lora_w = -(-lora_used // (4 * LANES)) * (4 * LANES)
    lora_pads = (lanes_up(d_lora) - d_lora, lanes_up(a_lora) - a_lora, lora_w - lora_lane[2] - g_lora)
    rkv_off = 2 * d
    lora_off = rkv_off + 3 * a_width
    bq_off = lora_off + lora_w
    assert lora_off % lora_w == 0

    def embed_rows(w, off, win):
        return jnp.pad(w, ((off - win[0], win[1] - off - w.shape[0]), (0, 0))).astype(BF16)

    windows = tuple((off, off + lanes_up(width)) for off, width in zip(lora_lane, lora_widths))
    lora_pieces = [piece for src, width, pad in zip(lora_src, lora_widths, lora_pads)
                   for piece in ((src, width), (None, pad)) if piece[1]]

    h = x.reshape(m, d)
    l = 0
    lambda_init = 0.8 - 0.6 * math.exp(-0.3 * l)
    w_in_r = _regroup_weight(
        jnp.swapaxes(w_in, 1, 2), l,
        [(o_g, w_in.shape[2] - o_g), (0, o_wd)] + lora_pieces + [(o_b, o_g - o_b)],
        tk=min(256, d))
    mu = mu_shift[l][None, :]
    mu_l = jnp.concatenate([_pad_cols(mu[:, src:src + width], width + pad)
                            for src, width, pad in zip(lora_src, lora_widths, lora_pads)], axis=1)
    plan = _tile_plan(m, s, w_in_r.shape[1], w_ff1.shape[2])

    z2 = _norm_matmul(h, norm_mix_g[l][None, :], w_in_r, tm=plan["inproj_tm"], tn=plan["inproj_tn"],
                      out_dtype=BF16)
    z3 = z2.reshape(b, s, z2.shape[1])

    ya = _rwkv_branch(
        z3, mu[:, :a_width], mu[:, a_width:2 * a_width], mu[:, 2 * a_width:3 * a_width], mu_l,
        w0[l][None, :], embed_rows(w_up[l], lora_lane[0], windows[0]), a0[l][None, :],
        embed_rows(a_up[l], lora_lane[1], windows[1]), embed_rows(g_up[l], lora_lane[2], windows[2]),
        k_k[l][None, :], k_a[l][None, :], r_k[l].reshape(1, a_width), lnx_g[l][None, :], lnx_b[l][None, :],
        a_width=a_width, rkv_block=rkv_off // a_width, lora_block=lora_off // lora_w, windows=windows,
        ts=plan["rwkv_ts"], out_dtype=BF16)

    ob = _attn_branch(
        z3, _bias_tiles(rel_bias, n_bheads), lambda_q1[l][None, :], lambda_k1[l][None, :],
        lambda_q2[l][None, :], lambda_k2[l][None, :], subln_g[l][:, None],
        q_block=bq_off // LANES, n_heads=n_bheads, lambda_init=lambda_init, out_dtype=BF16)

    h = _merge_out(ya.reshape(m, a_width), ob.reshape(m, b_width), z2, h, p_a[l].astype(BF16),
                   p_b[l].astype(BF16), w_out[l].astype(BF16), gate_block=0, tm=plan["merge_tm"])
    h = _ffn(h, norm_mlp_g[l][None, :], w_ff1[l].astype(BF16), w_ff2[l].astype(BF16),
             norm_final_g[None, :], tm=plan["ffn_tm"], tf=plan["ffn_tf"])
    return h.reshape(b, s, d)
```

```python
import functools
import math

import jax
import jax.numpy as jnp
from jax import lax
from jax.experimental import pallas as pl
from jax.experimental.pallas import tpu as pltpu

F32 = jnp.float32
BF16 = jnp.bfloat16

LANES = 128
HEAD = 64
CHUNK = 64
Q_BLOCK = 128
MXU_DIM = 256
BF16_ROWS = 16
KEY_TILE_BLOCKS = 2
RMS_EPS = 1e-6
GN_EPS = 64e-5
SUBLN_EPS = 1e-5
N_BUCKETS = 32
MAX_DISTANCE = 128
NEG_INF = -1e30
LOG2E = 1.4426950408889634
VMEM_LIMIT = 56 * 1024 * 1024


def _mm(a, b):
    return jnp.dot(a.astype(BF16), b.astype(BF16), preferred_element_type=F32)


def _mm_nt(a, b):
    return lax.dot_general(a.astype(BF16), b.astype(BF16), (((1,), (1,)), ((), ())),
                           preferred_element_type=F32)


def _params(sem):
    return pltpu.CompilerParams(dimension_semantics=sem, vmem_limit_bytes=VMEM_LIMIT)


def _norm_matmul_kernel(x_ref, g_ref, w_ref, o_ref, u_ref):
    @pl.when(pl.program_id(1) == 0)
    def _():
        x = x_ref[...]
        ms = jnp.mean(x * x, axis=-1, keepdims=True)
        u_ref[...] = (x * lax.rsqrt(ms + RMS_EPS) * g_ref[...]).astype(BF16)

    o_ref[...] = jnp.dot(u_ref[...], w_ref[...], preferred_element_type=F32).astype(o_ref.dtype)


def _norm_matmul(x, g, w, *, tm, tn, out_dtype):
    m, d = x.shape
    n = w.shape[1]
    return pl.pallas_call(
        _norm_matmul_kernel,
        grid=(m // tm, n // tn),
        in_specs=[pl.BlockSpec((tm, d), lambda i, j: (i, 0)),
                  pl.BlockSpec((1, d), lambda i, j: (0, 0)),
                  pl.BlockSpec((d, tn), lambda i, j: (0, j))],
        out_specs=pl.BlockSpec((tm, tn), lambda i, j: (i, j)),
        out_shape=jax.ShapeDtypeStruct((m, n), out_dtype),
        scratch_shapes=[pltpu.VMEM((tm, d), BF16)],
        compiler_params=_params(("parallel", "arbitrary")),
        name="norm_matmul",
    )(x, g, w)


def _expand(t, lane_lo):
    zero = jnp.zeros_like(t)
    return jnp.concatenate([jnp.where(lane_lo, t, zero), jnp.where(lane_lo, zero, t)], axis=0)


def _sum_heads(x, head_ones):
    xb = x.astype(BF16)
    g = head_ones.shape[0]
    return jnp.concatenate(
        [jnp.dot(xb[:, i * g:(i + 1) * g], head_ones, preferred_element_type=F32)
         for i in range(x.shape[1] // g)], axis=1)


def _chunk_cumsum(x, tri):
    hi = x.astype(BF16)
    lo = (x - hi.astype(F32)).astype(BF16)
    out = jnp.dot(tri, jnp.concatenate([hi, lo], axis=1), preferred_element_type=F32)
    return out[:, :x.shape[1]] + out[:, x.shape[1]:]


def _rwkv_kernel(zr_ref, zk_ref, zv_ref, zl_ref, mur_ref, muk_ref, muv_ref, mul_ref,
                 w0_ref, wup_ref, a0_ref, aup_ref, gup_ref, kk_ref, ka_ref, rk_ref, lng_ref, lnb_ref,
                 o_ref, cr_ref, ck_ref, cv_ref, cl_ref, h_ref, *, n_chunks, n_pairs, windows):
    c = CHUNK
    c2 = 2 * c
    ts = n_chunks * c
    units = [(p, ch) for ch in range(n_chunks) for p in range(n_pairs)]
    chunks = range(len(units))

    @pl.when(pl.program_id(1) == 0)
    def _():
        cr_ref[...] = jnp.zeros_like(cr_ref)
        ck_ref[...] = jnp.zeros_like(ck_ref)
        cv_ref[...] = jnp.zeros_like(cv_ref)
        cl_ref[...] = jnp.zeros_like(cl_ref)
        h_ref[...] = jnp.zeros_like(h_ref)

    def shifted(z, carry_ref, mu):
        rows = lax.broadcasted_iota(jnp.int32, z.shape, 0)
        prev = jnp.where(rows == 0, carry_ref[...], pltpu.roll(z, 1, 0))
        carry_ref[...] = z[z.shape[0] - 1:, :]
        return z + (prev - z) * mu

    ri = lax.broadcasted_iota(jnp.int32, (c2, c2), 0)
    ci = lax.broadcasted_iota(jnp.int32, (c2, c2), 1)
    strict = ci < ri
    incl = ci <= ri
    eye = (ci == ri).astype(F32)
    hr = lax.broadcasted_iota(jnp.int32, (MXU_DIM, MXU_DIM), 0)
    hc = lax.broadcasted_iota(jnp.int32, (MXU_DIM, MXU_DIM), 1)
    head_ones = ((hr // HEAD) == (hc // HEAD)).astype(BF16)
    ti = lax.broadcasted_iota(jnp.int32, (ts, ts), 0)
    tj = lax.broadcasted_iota(jnp.int32, (ts, ts), 1)
    tri = ((tj <= ti) & ((tj // c) == (ti // c))).astype(BF16)
    lane_lo = lax.broadcasted_iota(jnp.int32, (c, LANES), 1) < HEAD

    r = shifted(zr_ref[0].astype(F32), cr_ref, mur_ref[...])
    k = shifted(zk_ref[0].astype(F32), ck_ref, muk_ref[...])
    v = shifted(zv_ref[0].astype(F32), cv_ref, muv_ref[...])
    lo = shifted(zl_ref[0].astype(F32), cl_ref, mul_ref[...])

    (d0, d1), (a0_, a1_), (g0, g1) = windows
    t_w = w0_ref[...] + _mm(jnp.tanh(lo[:, d0:d1]), wup_ref[...])
    lw = -math.exp(-0.5) * jax.nn.sigmoid(t_w)
    a_sig = jax.nn.sigmoid(a0_ref[...] + _mm(lo[:, a0_:a1_], aup_ref[...]))
    gate = _mm(jax.nn.sigmoid(lo[:, g0:g1]), gup_ref[...])
    kk = k * kk_ref[...]
    kk = kk * lax.rsqrt(jnp.maximum(_sum_heads(kk * kk, head_ones), 1e-24))
    k2 = k * (1.0 + (a_sig - 1.0) * ka_ref[...])
    b_v = kk * a_sig
    bonus = _sum_heads(r * k2 * rk_ref[...], head_ones) * v
    cum = _chunk_cumsum(lw, tri)
    g_in = jnp.exp(cum)
    g_inv = jnp.exp(-cum)
    rt_all = r * g_in
    at_all = -kk * jnp.exp(cum - lw)
    kt_all = k2 * g_inv
    bt_all = b_v * g_inv

    def tile(x, u):
        p, ch = units[u]
        return x[ch * c:(ch + 1) * c, p * LANES:(p + 1) * LANES]

    cum_end = [tile(cum, u)[c - 1:, :] for u in chunks]
    g_tail = [jnp.exp(cum_end[u] - tile(cum, u)) for u in chunks]
    rt = [_expand(tile(rt_all, u), lane_lo) for u in chunks]
    at = [_expand(tile(at_all, u), lane_lo) for u in chunks]
    kt = [_expand(tile(kt_all, u), lane_lo) for u in chunks]
    bt = [_expand(tile(bt_all, u), lane_lo) for u in chunks]
    kb = [_expand(tile(k2, u) * g_tail[u], lane_lo) for u in chunks]
    bb = [_expand(tile(b_v, u) * g_tail[u], lane_lo) for u in chunks]
    ve = [_expand(tile(v, u), lane_lo) for u in chunks]

    sc = [_mm_nt(jnp.concatenate([at[ch], rt[ch]], axis=0), jnp.concatenate([bt[ch], kt[ch]], axis=0))
          for ch in chunks]
    n_ab = [jnp.where(strict, sc[ch][:c2, :c2], 0.0) for ch in chunks]
    a_ak = [jnp.where(strict, sc[ch][:c2, c2:], 0.0) for ch in chunks]
    a_rb = [jnp.where(incl, sc[ch][c2:, :c2], 0.0) for ch in chunks]
    a_rk = [jnp.where(incl, sc[ch][c2:, c2:], 0.0) for ch in chunks]

    p = [eye + n_ab[ch] for ch in chunks]
    nk = [_mm(n_ab[ch], n_ab[ch]) for ch in chunks]
    akv = [_mm(a_ak[ch], ve[ch]) for ch in chunks]
    power = 2
    while power < c:
        if 2 * power < c:
            both = [_mm(nk[ch], jnp.concatenate([nk[ch], p[ch]], axis=1)) for ch in chunks]
            nk = [both[ch][:, :c2] for ch in chunks]
            p = [p[ch] + both[ch][:, c2:] for ch in chunks]
        else:
            p = [p[ch] + _mm(nk[ch], p[ch]) for ch in chunks]
        power *= 2

    ta = [_mm(p[ch], jnp.concatenate([at[ch], akv[ch]], axis=1)) for ch in chunks]
    zero = jnp.zeros((c2, LANES), F32)
    rhs = [jnp.concatenate([ta[ch], jnp.concatenate([zero, ve[ch]], axis=1)], axis=0) for ch in chunks]
    lhs = [jnp.concatenate([jnp.concatenate([a_rb[ch], a_rk[ch]], axis=1),
                            jnp.concatenate([bb[ch].T, kb[ch].T], axis=1)], axis=0) for ch in chunks]
    out = [_mm(lhs[ch], rhs[ch]) for ch in chunks]
    r_hat = [rt[ch] + out[ch][:c2, :LANES] for ch in chunks]
    y_in = [out[ch][:c2, LANES:] for ch in chunks]
    m_state = [eye * jnp.exp(cum_end[ch]) + out[ch][c2:, :LANES] for ch in chunks]
    g_state = [out[ch][c2:, LANES:] for ch in chunks]

    h = [h_ref[p] for p in range(n_pairs)]
    ys = [[] for _ in range(n_pairs)]
    for u in chunks:
        p = units[u][0]
        both = _mm(jnp.concatenate([r_hat[u], m_state[u]], axis=0), h[p])
        ye = both[:c2] + y_in[u]
        h[p] = both[c2:] + g_state[u]
        ys[p].append(ye[:c] + ye[c:])
    for p in range(n_pairs):
        h_ref[p] = h[p]
    y = jnp.concatenate([jnp.concatenate(ys[p], axis=0) for p in range(n_pairs)], axis=1)

    mean = _sum_heads(y, head_ones) * (1.0 / HEAD)
    d = y - mean
    var = _sum_heads(d * d, head_ones) * (1.0 / HEAD)
    yn = d * lax.rsqrt(var + GN_EPS) * lng_ref[...] + lnb_ref[...]
    o_ref[0] = ((yn + bonus) * gate).astype(o_ref.dtype)


def _rwkv_branch(z3, mu_r, mu_k, mu_v, mu_l, w0, wup, a0, aup, gup, k_k, k_a, r_k, lnx_g, lnx_b,
                 *, a_width, rkv_block, lora_block, windows, ts, out_dtype):
    b, s, _ = z3.shape
    n_pairs = a_width // LANES
    lw = mu_l.shape[1]

    def col(blk):
        return lambda bi, si: (bi, si, blk)

    def full(rows, width):
        return pl.BlockSpec((rows, width), lambda bi, si: (0, 0))

    in_specs = [
        pl.BlockSpec((1, ts, a_width), col(rkv_block)),
        pl.BlockSpec((1, ts, a_width), col(rkv_block + 1)),
        pl.BlockSpec((1, ts, a_width), col(rkv_block + 2)),
        pl.BlockSpec((1, ts, lw), col(lora_block)),
        full(1, a_width), full(1, a_width), full(1, a_width), full(1, lw),
        full(1, a_width), full(wup.shape[0], a_width), full(1, a_width), full(aup.shape[0], a_width),
        full(gup.shape[0], a_width),
        full(1, a_width), full(1, a_width), full(1, a_width), full(1, a_width), full(1, a_width),
    ]
    return pl.pallas_call(
        functools.partial(_rwkv_kernel, n_chunks=ts // CHUNK, n_pairs=n_pairs, windows=windows),
        grid=(b, s // ts),
        in_specs=in_specs,
        out_specs=pl.BlockSpec((1, ts, a_width), lambda bi, si: (bi, si, 0)),
        out_shape=jax.ShapeDtypeStruct((b, s, a_width), out_dtype),
        scratch_shapes=[pltpu.VMEM((1, a_width), F32), pltpu.VMEM((1, a_width), F32),
                        pltpu.VMEM((1, a_width), F32), pltpu.VMEM((1, lw), F32),
                        pltpu.VMEM((n_pairs, LANES, LANES), F32)],
        compiler_params=_params(("parallel", "arbitrary")),
        name="rwkv7_chunked",
    )(z3, z3, z3, z3, mu_r, mu_k, mu_v, mu_l, w0, wup, a0, aup, gup, k_k, k_a, r_k, lnx_g, lnx_b)


def _attn_kernel(q_ref, k_ref, v_ref, bias_ref, lq1_ref, lk1_ref, lq2_ref, lk2_ref, sg_ref, o_ref,
                 s_ref, p_ref, vt_ref, *, n_qblocks, lambda_init):
    qr = Q_BLOCK
    lam = (jnp.exp(jnp.sum(lq1_ref[...] * lk1_ref[...])) - jnp.exp(jnp.sum(lq2_ref[...] * lk2_ref[...]))
           + lambda_init)
    lane_lo = lax.broadcasted_iota(jnp.int32, (qr, LANES), 1) < HEAD
    scale = HEAD ** -0.5 * LOG2E
    n_ones = vt_ref.shape[0] - LANES
    n_buf = s_ref.shape[0]

    def key_tiles(qb):
        return [(j, min(KEY_TILE_BLOCKS, qb + 1 - j)) for j in range(0, qb + 1, KEY_TILE_BLOCKS)]

    def logits(qb):
        rows = slice(qb * qr, (qb + 1) * qr)
        vt_ref[:LANES, rows] = v_ref[0, rows, :].astype(F32).T.astype(BF16)
        vt_ref[LANES:, rows] = jnp.ones((n_ones, qr), BF16)
        q = q_ref[0, rows, :].astype(F32) * scale
        qe_t = _expand(q, lane_lo).T.astype(BF16)
        m = None
        for j0, nb in key_tiles(qb):
            s = jnp.dot(k_ref[0, j0 * qr:(j0 + nb) * qr, :], qe_t, preferred_element_type=F32)
            for t in range(nb):
                j = j0 + t
                sj = s[t * qr:(t + 1) * qr, :]
                if qb - j < 2:
                    sj = sj + bias_ref[0, qb - j]
                s_ref[qb % n_buf, j * qr:(j + 1) * qr, :] = sj
                m_j = jnp.max(sj, axis=0, keepdims=True)
                m = m_j if m is None else jnp.maximum(m, m_j)
        return m

    def weighted_values(qb, m):
        for j in range(qb + 1):
            pj = jnp.exp2(s_ref[qb % n_buf, j * qr:(j + 1) * qr, :] - m)
            p_ref[qb % n_buf, j * qr:(j + 1) * qr, :] = pj.astype(BF16)
        n_keys = (qb + 1) * qr
        acc = jnp.dot(vt_ref[:, :n_keys], p_ref[qb % n_buf, :n_keys, :], preferred_element_type=F32)
        o = acc[:LANES] / acc[LANES:LANES + 1]
        o = o[:, :qr] - lam * o[:, qr:]
        o = o * lax.rsqrt(jnp.mean(o * o, axis=0, keepdims=True) + SUBLN_EPS) * sg_ref[...]
        o = o * (1.0 - lambda_init)
        o_ref[0, qb * qr:(qb + 1) * qr, :] = o.T.astype(o_ref.dtype)

    ahead = n_buf - 1
    pending = [logits(qb) for qb in range(min(ahead, n_qblocks))]
    for qb in range(n_qblocks):
        if qb + ahead < n_qblocks:
            pending.append(logits(qb + ahead))
        weighted_values(qb, pending.pop(0))


def _attn_branch(z3, bias, lq1, lk1, lq2, lk2, subln_g, *, q_block, n_heads, lambda_init, out_dtype):
    b, s, _ = z3.shape

    def col(off):
        return lambda bi, hh: (bi, 0, off + hh)

    def small(n):
        return pl.BlockSpec((1, n), lambda bi, hh: (0, 0))

    return pl.pallas_call(
        functools.partial(_attn_kernel, n_qblocks=s // Q_BLOCK, lambda_init=lambda_init),
        grid=(b, n_heads),
        in_specs=[pl.BlockSpec((1, s, LANES), col(q_block)),
                  pl.BlockSpec((1, s, LANES), col(q_block + n_heads)),
                  pl.BlockSpec((1, s, LANES), col(q_block + 2 * n_heads)),
                  pl.BlockSpec((1, 2, Q_BLOCK, 2 * Q_BLOCK), lambda bi, hh: (hh, 0, 0, 0)),
                  small(HEAD), small(HEAD), small(HEAD), small(HEAD),
                  pl.BlockSpec((LANES, 1), lambda bi, hh: (0, 0))],
        out_specs=pl.BlockSpec((1, s, LANES), lambda bi, hh: (bi, 0, hh)),
        out_shape=jax.ShapeDtypeStruct((b, s, n_heads * LANES), out_dtype),
        scratch_shapes=[pltpu.VMEM((3, s, 2 * Q_BLOCK), F32), pltpu.VMEM((3, s, 2 * Q_BLOCK), BF16),
                        pltpu.VMEM((LANES + BF16_ROWS, s), BF16)],
        compiler_params=_params(("parallel", "parallel")),
        name="diff_attention",
    )(z3, z3, z3, bias, lq1, lk1, lq2, lk2, subln_g)


def _t5_bucket(rel):
    nb = N_BUCKETS // 2
    max_exact = nb // 2
    ret = jnp.where(rel > 0, nb, 0)
    n = jnp.abs(rel)
    nf = jnp.maximum(n, 1).astype(jnp.float32)
    large = max_exact + (jnp.log(nf / max_exact) / math.log(MAX_DISTANCE / max_exact)
                         * (nb - max_exact)).astype(jnp.int32)
    large = jnp.minimum(large, nb - 1)
    return ret + jnp.where(n < max_exact, n, large)


def _bias_tiles(rel_bias, n_heads):
    assert Q_BLOCK >= MAX_DISTANCE
    dist = jnp.arange(3)[:, None, None]
    qi = jnp.arange(Q_BLOCK)[None, :, None]
    kj = jnp.arange(Q_BLOCK)[None, None, :]
    bucket = _t5_bucket(kj - qi - dist * Q_BLOCK)
    onehot = (bucket[..., None] == jnp.arange(N_BUCKETS)).astype(F32)
    t = jnp.einsum("dqkn,nh->hdqk", onehot, rel_bias.astype(F32), precision=lax.Precision.HIGHEST)
    allowed = (dist > 0) | ((kj // CHUNK) <= (qi // CHUNK))
    t = jnp.where(allowed[None], t, NEG_INF)
    t = (t[:, :2] - t[:, 2:]) * LOG2E
    t = t.reshape(n_heads, 2, 2, Q_BLOCK, Q_BLOCK).transpose(0, 2, 4, 1, 3)
    return t.reshape(n_heads, 2, Q_BLOCK, 2 * Q_BLOCK)


def _merge_out_kernel(ya_ref, ob_ref, ga_ref, gb_ref, x_ref, pa_ref, pb_ref, wo_ref, o_ref, mg_ref):
    half = mg_ref.shape[1] // 2
    for c0 in (0, half):
        cols = slice(c0, c0 + half)
        oa = jnp.dot(ya_ref[...], pa_ref[:, cols], preferred_element_type=F32)
        ob = jnp.dot(ob_ref[...], pb_ref[:, cols], preferred_element_type=F32)
        mg_ref[:, cols] = (jax.nn.sigmoid(ga_ref[:, cols].astype(F32)) * oa
                           + jax.nn.sigmoid(gb_ref[:, cols].astype(F32)) * ob).astype(BF16)
    o_ref[...] = x_ref[...] + jnp.dot(mg_ref[...], wo_ref[...], preferred_element_type=F32)


def _merge_out(ya, ob, z2, x, p_a, p_b, w_out, *, gate_block, tm):
    m, ka = ya.shape
    kb = ob.shape[1]
    d = w_out.shape[1]

    def resident(shape):
        return pl.BlockSpec(shape, lambda i: (0, 0), pipeline_mode=pl.Buffered(1))

    return pl.pallas_call(
        _merge_out_kernel,
        grid=(m // tm,),
        in_specs=[pl.BlockSpec((tm, ka), lambda i: (i, 0)),
                  pl.BlockSpec((tm, kb), lambda i: (i, 0)),
                  pl.BlockSpec((tm, d), lambda i: (i, gate_block)),
                  pl.BlockSpec((tm, d), lambda i: (i, gate_block + 1)),
                  pl.BlockSpec((tm, d), lambda i: (i, 0)),
                  resident(p_a.shape), resident(p_b.shape), resident(w_out.shape)],
        out_specs=pl.BlockSpec((tm, d), lambda i: (i, 0)),
        out_shape=jax.ShapeDtypeStruct((m, d), F32),
        scratch_shapes=[pltpu.VMEM((tm, d), BF16)],
        compiler_params=_params(("parallel",)),
        name="merge_out_proj",
    )(ya, ob, z2, z2, x, p_a, p_b, w_out)


def _ffn_kernel(h_ref, g1_ref, w1_ref, w2_ref, g2_ref, o_ref, m_ref):
    j = pl.program_id(1)

    @pl.when(j == 0)
    def _():
        x = h_ref[...]
        ms = jnp.mean(x * x, axis=-1, keepdims=True)
        m_ref[...] = (x * lax.rsqrt(ms + RMS_EPS) * g1_ref[...]).astype(BF16)
        o_ref[...] = x

    f = jnp.dot(m_ref[...], w1_ref[...], preferred_element_type=F32)
    f = jnp.square(jnp.maximum(f, 0.0)).astype(BF16)
    o_ref[...] += jnp.dot(f, w2_ref[...], preferred_element_type=F32)

    @pl.when(j == pl.num_programs(1) - 1)
    def _():
        y = o_ref[...]
        ms = jnp.mean(y * y, axis=-1, keepdims=True)
        o_ref[...] = y * lax.rsqrt(ms + RMS_EPS) * g2_ref[...]


def _ffn(h, g1, w1, w2, g2, *, tm, tf):
    m, d = h.shape
    dff = w1.shape[1]
    return pl.pallas_call(
        _ffn_kernel,
        grid=(m // tm, dff // tf),
        in_specs=[pl.BlockSpec((tm, d), lambda i, j: (i, 0)),
                  pl.BlockSpec((1, d), lambda i, j: (0, 0)),
                  pl.BlockSpec((d, tf), lambda i, j: (0, j)),
                  pl.BlockSpec((tf, d), lambda i, j: (j, 0)),
                  pl.BlockSpec((1, d), lambda i, j: (0, 0))],
        out_specs=pl.BlockSpec((tm, d), lambda i, j: (i, 0)),
        out_shape=jax.ShapeDtypeStruct((m, d), F32),
        scratch_shapes=[pltpu.VMEM((tm, d), BF16)],
        compiler_params=_params(("parallel", "arbitrary")),
        name="ffn_residual_norm",
    )(h, g1, w1, w2, g2)


def _pad_cols(w, width):
    return jnp.pad(w, ((0, 0), (0, width - w.shape[1])))


def _regroup_kernel(wt_ref, o_ref, *, pieces):
    col = 0
    for src, width in pieces:
        for c0 in range(0, width, LANES):
            cw = min(LANES, width - c0)
            if src is None:
                o_ref[:, col + c0:col + c0 + cw] = jnp.zeros((o_ref.shape[0], cw), o_ref.dtype)
            else:
                assert src + c0 + LANES <= wt_ref.shape[1]
                blk = wt_ref[0, src + c0:src + c0 + LANES, :].T
                o_ref[:, col + c0:col + c0 + cw] = blk[:, :cw].astype(o_ref.dtype)
        col += width


def _regroup_weight(wt, layer, pieces, *, tk):
    _, n, k = wt.shape
    out_cols = sum(width for _, width in pieces)
    return pl.pallas_call(
        functools.partial(_regroup_kernel, pieces=pieces),
        grid=(k // tk,),
        in_specs=[pl.BlockSpec((1, n, tk), lambda i: (layer, 0, i))],
        out_specs=pl.BlockSpec((tk, out_cols), lambda i: (i, 0)),
        out_shape=jax.ShapeDtypeStruct((k, out_cols), BF16),
        compiler_params=_params(("parallel",)),
        name="regroup_weight",
    )(wt)


def _tile_plan(m, s, z_cols, d_ff):
    plan = dict(
        inproj_tm=min(1024, m), inproj_tn=1536,
        rwkv_ts=4 * CHUNK,
        merge_tm=min(512, m),
        ffn_tm=min(512, m), ffn_tf=2048)
    assert m % plan["inproj_tm"] == 0 and z_cols % plan["inproj_tn"] == 0
    assert s % plan["rwkv_ts"] == 0 and s % (2 * Q_BLOCK) == 0
    assert m % plan["merge_tm"] == 0 and m % plan["ffn_tm"] == 0 and d_ff % plan["ffn_tf"] == 0
    return plan


def kernel(x, norm_mix_g, w_in, mu_shift, w0, w_up, a0, a_up, g_up, k_k, k_a, r_k, lnx_g, lnx_b, lambda_q1, lambda_k1, lambda_q2, lambda_k2, subln_g, rel_bias, p_a, p_b, w_out, norm_mlp_g, w_ff1, w_ff2, norm_final_g):
    b, s, d = x.shape
    depth = w_in.shape[0]
    a_width = w0.shape[1]
    d_lora, a_lora, g_lora = w_up.shape[1], a_up.shape[1], g_up.shape[1]
    n_bheads = rel_bias.shape[1] // 2
    b_width = n_bheads * LANES
    assert depth == 1, "the final norm is fused into the (single) layer's MLP kernel"
    assert a_width % MXU_DIM == 0 and subln_g.shape[1] == LANES
    m = b * s

    o_wd = 3 * a_width
    o_b = o_wd + d_lora + a_lora + g_lora
    o_g = o_b + 3 * b_width
    def lanes_up(n):
        return -(-n // LANES) * LANES

    lora_widths = (d_lora, a_lora, g_lora)
    lora_src = (o_wd, o_wd + d_lora, o_wd + d_lora + a_lora)
    lora_lane = (0, lanes_up(d_lora), lanes_up(d_lora) + lanes_up(a_lora))
    lora_used = lora_lane[2] + lanes_up(g_lora)
    lora_w = -(-lora_used // (4 * LANES)) * (4 * LANES)
    lora_pads = (lanes_up(d_lora) - d_lora, lanes_up(a_lora) - a_lora, lora_w - lora_lane[2] - g_lora)
    rkv_off = 2 * d
    lora_off = rkv_off + 3 * a_width
    bq_off = lora_off + lora_w
    assert lora_off % lora_w == 0

    def embed_rows(w, off, win):
        return jnp.pad(w, ((off - win[0], win[1] - off - w.shape[0]), (0, 0))).astype(BF16)

    windows = tuple((off, off + lanes_up(width)) for off, width in zip(lora_lane, lora_widths))
    lora_pieces = [piece for src, width, pad in zip(lora_src, lora_widths, lora_pads)
                   for piece in ((src, width), (None, pad)) if piece[1]]

    h = x.reshape(m, d)
    l = 0
    lambda_init = 0.8 - 0.6 * math.exp(-0.3 * l)
    w_in_r = _regroup_weight(
        jnp.swapaxes(w_in, 1, 2), l,
        [(o_g, w_in.shape[2] - o_g), (0, o_wd)] + lora_pieces + [(o_b, o_g - o_b)],
        tk=min(256, d))
    mu = mu_shift[l][None, :]
    mu_l = jnp.concatenate([_pad_cols(mu[:, src:src + width], width + pad)
                            for src, width, pad in zip(lora_src, lora_widths, lora_pads)], axis=1)
    plan = _tile_plan(m, s, w_in_r.shape[1], w_ff1.shape[2])

    z2 = _norm_matmul(h, norm_mix_g[l][None, :], w_in_r, tm=plan["inproj_tm"], tn=plan["inproj_tn"],
                      out_dtype=BF16)
    z3 = z2.reshape(b, s, z2.shape[1])

    ya = _rwkv_branch(
        z3, mu[:, :a_width], mu[:, a_width:2 * a_width], mu[:, 2 * a_width:3 * a_width], mu_l,
        w0[l][None, :], embed_rows(w_up[l], lora_lane[0], windows[0]), a0[l][None, :],
        embed_rows(a_up[l], lora_lane[1], windows[1]), embed_rows(g_up[l], lora_lane[2], windows[2]),
        k_k[l][None, :], k_a[l][None, :], r_k[l].reshape(1, a_width), lnx_g[l][None, :], lnx_b[l][None, :],
        a_width=a_width, rkv_block=rkv_off // a_width, lora_block=lora_off // lora_w, windows=windows,
        ts=plan["rwkv_ts"], out_dtype=BF16)

    ob = _attn_branch(
        z3, _bias_tiles(rel_bias, n_bheads), lambda_q1[l][None, :], lambda_k1[l][None, :],
        lambda_q2[l][None, :], lambda_k2[l][None, :], subln_g[l][:, None],
        q_block=bq_off // LANES, n_heads=n_bheads, lambda_init=lambda_init, out_dtype=BF16)

    h = _merge_out(ya.reshape(m, a_width), ob.reshape(m, b_width), z2, h, p_a[l].astype(BF16),
                   p_b[l].astype(BF16), w_out[l].astype(BF16), gate_block=0, tm=plan["merge_tm"])
    h = _ffn(h, norm_mlp_g[l][None, :], w_ff1[l].astype(BF16), w_ff2[l].astype(BF16),
             norm_final_g[None, :], tm=plan["ffn_tm"], tf=plan["ffn_tf"])
    return h.reshape(b, s, d)
```
